```python
import math
import jax, jax.numpy as jnp
from jax import lax
import numpy as np

D_MODEL = 4096
BATCH = 4
SEQ = 2048
DEPTH = 2
DEC_BATCH = 32
DEC_SEQ = 16
PAST_LEN = 4096

CHUNK = 64
EPS = 1e-6
D_FF = 11008
POOL_WINDOWS = (2, 4, 8, 16)
POOL_GROUP = D_MODEL // 16
POOL_WIDTH = POOL_GROUP * len(POOL_WINDOWS)
POOL_MAX = max(POOL_WINDOWS)
POOL_HIST = POOL_MAX - 1
SWA_WINDOW = 128
SWA_HEAD_DIM = 64
SWA_HEADS = D_MODEL // 128
SWA_KV_HEADS = SWA_HEADS // 8
SWA_GROUP = SWA_HEADS // SWA_KV_HEADS
SWA_WIDTH = SWA_HEADS * SWA_HEAD_DIM
SWA_KV_WIDTH = SWA_KV_HEADS * SWA_HEAD_DIM
BAND_CHUNKS = SWA_WINDOW // CHUNK + 1
N_MEM = 256
MEM_HEADS = 4
MEM_WIDTH = D_MODEL // 4
MEM_HEAD_DIM = MEM_WIDTH // MEM_HEADS
N_BRANCH = 3
IN_WIDTH = POOL_WIDTH + SWA_WIDTH + 2 * SWA_KV_WIDTH + MEM_WIDTH + N_BRANCH * D_MODEL
REL_BUCKETS = 32
REL_MAX_DIST = 128
NEG = -1e30

kernel_name = 'hybrid_pool_swa_mem_macaron_step'


def rms_norm(x, g):
    xf = x.astype(jnp.float32)
    y = xf * lax.rsqrt(jnp.mean(xf * xf, axis=-1, keepdims=True) + EPS)
    return (y * g.astype(jnp.float32)).astype(x.dtype)


def swiglu_half(x, g, w_in, w_out):
    gate, up = jnp.split(rms_norm(x, g) @ w_in, 2, axis=-1)
    return x + 0.5 * ((jax.nn.silu(gate) * up) @ w_out)


def rel_bucket(rel):
    half = REL_BUCKETS // 2
    max_exact = half // 2
    n = jnp.abs(rel)
    nf = jnp.maximum(n, 1).astype(jnp.float32)
    large = max_exact + (jnp.log(nf / max_exact) / math.log(REL_MAX_DIST / max_exact)
                         * (half - max_exact)).astype(jnp.int32)
    large = jnp.minimum(large, half - 1)
    return jnp.where(rel > 0, half, 0) + jnp.where(n < max_exact, n, large)


def rel_bias_for(rel, table):
    b = table.astype(jnp.float32)[rel_bucket(rel)]
    b = jnp.transpose(b, (2, 0, 1))
    return b.reshape(SWA_KV_HEADS, SWA_GROUP, rel.shape[0], rel.shape[1])


def sink_softmax(s, sinks):
    sk = sinks.astype(jnp.float32).reshape(SWA_KV_HEADS, SWA_GROUP)[:, :, None, None]
    sk = jnp.broadcast_to(sk, s.shape[:-1] + (1,))
    return jax.nn.softmax(jnp.concatenate([s, sk], axis=-1), axis=-1)[..., :-1]


def split_proj(z):
    o1 = POOL_WIDTH
    o2 = o1 + SWA_WIDTH
    o3 = o2 + SWA_KV_WIDTH
    o4 = o3 + SWA_KV_WIDTH
    o5 = o4 + MEM_WIDTH
    return jnp.split(z, [o1, o2, o3, o4, o5], axis=-1)


def pool_mixer(u, hist, w_group, scale):
    B, S, _ = u.shape
    full = jnp.concatenate([hist, u], axis=1)
    n_hist = hist.shape[1]
    n_full = n_hist + S
    csp = jnp.pad(lax.cumsum(full.astype(jnp.float32), axis=1), ((0, 0), (POOL_MAX, 0), (0, 0)))
    rows = jnp.arange(n_hist, n_full)
    outs = []
    for g, w in enumerate(POOL_WINDOWS):
        c = csp[:, :, g * POOL_GROUP:(g + 1) * POOL_GROUP]
        win = c[:, n_hist + POOL_MAX:n_full + POOL_MAX] - c[:, n_hist + POOL_MAX - w:n_full + POOL_MAX - w]
        cnt = jnp.minimum(rows + 1, w).astype(jnp.float32)[None, :, None]
        outs.append(win / cnt - u[:, :, g * POOL_GROUP:(g + 1) * POOL_GROUP].astype(jnp.float32))
    mixed = jnp.stack(outs, axis=2).astype(u.dtype)
    y = jnp.einsum('bsgc,gcd->bsgd', mixed, w_group).reshape(B, S, POOL_WIDTH)
    return y * scale, full[:, n_full - POOL_HIST:]


def swa_prompt(q, k, v, sinks, table):
    B, S, _ = q.shape
    nc = S // CHUNK
    qc = q.reshape(B, nc, CHUNK, SWA_KV_HEADS, SWA_GROUP, SWA_HEAD_DIM)

    def band(t):
        tp = jnp.pad(t.reshape(B, S, SWA_KV_HEADS, SWA_HEAD_DIM), ((0, 0), (SWA_WINDOW, 0), (0, 0), (0, 0)))
        tp = tp.reshape(B, nc + BAND_CHUNKS - 1, CHUNK, SWA_KV_HEADS, SWA_HEAD_DIM)
        return jnp.concatenate([tp[:, i:i + nc] for i in range(BAND_CHUNKS)], axis=2)

    kb, vb = band(k), band(v)
    s = jnp.einsum('bnqhgd,bnkhd->bnhgqk', qc, kb).astype(jnp.float32) * SWA_HEAD_DIM ** -0.5
    nk = BAND_CHUNKS * CHUNK
    rel = jnp.arange(nk)[None, :] - SWA_WINDOW - jnp.arange(CHUNK)[:, None]
    s = s + rel_bias_for(rel, table)
    kpos = jnp.arange(nc)[:, None] * CHUNK - SWA_WINDOW + jnp.arange(nk)[None, :]
    s = jnp.where((kpos >= 0)[None, :, None, None, None, :], s, NEG)
    p = sink_softmax(s, sinks).astype(vb.dtype)
    o = jnp.einsum('bnhgqk,bnkhd->bnqhgd', p, vb)
    return o.reshape(B, S, SWA_WIDTH)


def swa_sample(q, k_new, v_new, k_cache, v_cache, sinks, table):
    B, S, _ = q.shape
    n_cache = k_cache.shape[1]
    qh = q.reshape(B, S, SWA_KV_HEADS, SWA_GROUP, SWA_HEAD_DIM)
    kk = jnp.concatenate([k_cache, k_new.reshape(B, S, SWA_KV_HEADS, SWA_HEAD_DIM)], axis=1)
    vv = jnp.concatenate([v_cache, v_new.reshape(B, S, SWA_KV_HEADS, SWA_HEAD_DIM)], axis=1)
    s = jnp.einsum('bqhgd,bkhd->bhgqk', qh, kk).astype(jnp.float32) * SWA_HEAD_DIM ** -0.5
    rel = jnp.arange(n_cache + S)[None, :] - n_cache - jnp.arange(S)[:, None]
    s = s + rel_bias_for(rel, table)
    p = sink_softmax(s, sinks).astype(vv.dtype)
    o = jnp.einsum('bhgqk,bkhd->bqhgd', p, vv).reshape(B, S, SWA_WIDTH)
    return o, kk[:, S:], vv[:, S:]


def mem_kv(mem, g, w):
    B = mem.shape[0]
    k, v = jnp.split(rms_norm(mem, g) @ w, 2, axis=-1)
    return (k.reshape(B, N_MEM, MEM_HEADS, MEM_HEAD_DIM),
            v.reshape(B, N_MEM, MEM_HEADS, MEM_HEAD_DIM))


def mem_attend(mq, k, v):
    B, S, _ = mq.shape
    q = mq.reshape(B, S, MEM_HEADS, MEM_HEAD_DIM)
    s = jnp.einsum('bqhd,bkhd->bhqk', q, k).astype(jnp.float32) * MEM_HEAD_DIM ** -0.5
    p = jax.nn.softmax(s, axis=-1).astype(v.dtype)
    return jnp.einsum('bhqk,bkhd->bqhd', p, v).reshape(B, S, MEM_WIDTH)


def merge(gates, y_pool, y_swa, y_mem, w_bp, w_bs, w_bm, w_o):
    g = jax.nn.sigmoid(gates.astype(jnp.float32)).astype(gates.dtype)
    g_pool, g_swa, g_mem = jnp.split(g, N_BRANCH, axis=-1)
    m = g_pool * (y_pool @ w_bp) + g_swa * (y_swa @ w_bs) + g_mem * (y_mem @ w_bm)
    return m @ w_o


def setup_inputs(seed: int = 0) -> dict:
    key = jax.random.key(seed)
    ks = jax.random.split(key, 28)
    f32 = jnp.float32

    def nrm(k, shape, scale=1.0):
        return jax.random.normal(k, shape, f32) * scale

    def gain(k, shape):
        return 1.0 + 0.01 * jax.random.normal(k, shape, f32)

    swa_cache = min(SWA_WINDOW, PAST_LEN)
    return {
        'x_prompt': nrm(ks[0], (BATCH, SEQ, D_MODEL)),
        'x_sample': nrm(ks[1], (DEC_BATCH, DEC_SEQ, D_MODEL)),
        'cache_swa_k': nrm(ks[2], (DEPTH, DEC_BATCH, swa_cache, SWA_KV_HEADS, SWA_HEAD_DIM)),
        'cache_swa_v': nrm(ks[3], (DEPTH, DEC_BATCH, swa_cache, SWA_KV_HEADS, SWA_HEAD_DIM)),
        'state_pool': nrm(ks[4], (DEPTH, DEC_BATCH, POOL_HIST, POOL_WIDTH)),
        'cache_mem_k': nrm(ks[5], (DEPTH, DEC_BATCH, N_MEM, MEM_HEADS, MEM_HEAD_DIM)),
        'cache_mem_v': nrm(ks[6], (DEPTH, DEC_BATCH, N_MEM, MEM_HEADS, MEM_HEAD_DIM)),
        'mem_prompt': nrm(ks[7], (BATCH, N_MEM, D_MODEL)),
        'ffn_a_norm': gain(ks[8], (DEPTH, D_MODEL)),
        'ffn_a_w_in': nrm(ks[9], (DEPTH, D_MODEL, 2 * D_FF), D_MODEL ** -0.5),
        'ffn_a_w_out': nrm(ks[10], (DEPTH, D_FF, D_MODEL), D_FF ** -0.5),
        'mix_norm': gain(ks[11], (DEPTH, D_MODEL)),
        'mem_norm': gain(ks[12], (DEPTH, D_MODEL)),
        'w_in': nrm(ks[13], (DEPTH, D_MODEL, IN_WIDTH), D_MODEL ** -0.5),
        'pool_w': nrm(ks[14], (DEPTH, len(POOL_WINDOWS), POOL_GROUP, POOL_GROUP), POOL_GROUP ** -0.5),
        'pool_scale': 1.0 + 0.1 * jax.random.normal(ks[15], (DEPTH, POOL_WIDTH), f32),
        'swa_sinks': nrm(ks[16], (DEPTH, SWA_HEADS)),
        'rel_bias': nrm(ks[17], (REL_BUCKETS, SWA_HEADS), 0.5),
        'w_mem_kv': nrm(ks[18], (DEPTH, D_MODEL, 2 * MEM_WIDTH), D_MODEL ** -0.5),
        'w_branch_pool': nrm(ks[19], (DEPTH, POOL_WIDTH, D_MODEL), POOL_WIDTH ** -0.5),
        'w_branch_swa': nrm(ks[20], (DEPTH, SWA_WIDTH, D_MODEL), SWA_WIDTH ** -0.5),
        'w_branch_mem': nrm(ks[21], (DEPTH, MEM_WIDTH, D_MODEL), MEM_WIDTH ** -0.5),
        'w_out': nrm(ks[22], (DEPTH, D_MODEL, D_MODEL), D_MODEL ** -0.5),
        'ffn_b_norm': gain(ks[23], (DEPTH, D_MODEL)),
        'ffn_b_w_in': nrm(ks[24], (DEPTH, D_MODEL, 2 * D_FF), D_MODEL ** -0.5),
        'ffn_b_w_out': nrm(ks[25], (DEPTH, D_FF, D_MODEL), D_FF ** -0.5),
        'final_norm': gain(ks[26], (D_MODEL,)),
    }


def reference(x_prompt, x_sample, cache_swa_k, cache_swa_v, state_pool, cache_mem_k, cache_mem_v,
              mem_prompt, ffn_a_norm, ffn_a_w_in, ffn_a_w_out, mix_norm, mem_norm, w_in, pool_w,
              pool_scale, swa_sinks, rel_bias, w_mem_kv, w_branch_pool, w_branch_swa, w_branch_mem,
              w_out, ffn_b_norm, ffn_b_w_in, ffn_b_w_out, final_norm):
    xp, xs = x_prompt, x_sample
    Bp = xp.shape[0]
    p_k, p_v, p_pool, p_mk, p_mv = [], [], [], [], []
    s_k, s_v, s_pool = [], [], []
    for l in range(DEPTH):
        xp = swiglu_half(xp, ffn_a_norm[l], ffn_a_w_in[l], ffn_a_w_out[l])
        xs = swiglu_half(xs, ffn_a_norm[l], ffn_a_w_in[l], ffn_a_w_out[l])

        u, q, k, v, mq, gates = split_proj(rms_norm(xp, mix_norm[l]) @ w_in[l])
        y_pool, pool_last = pool_mixer(u, u[:, :0], pool_w[l], pool_scale[l])
        y_swa = swa_prompt(q, k, v, swa_sinks[l], rel_bias)
        mk, mv = mem_kv(mem_prompt, mem_norm[l], w_mem_kv[l])
        y_mem = mem_attend(mq, mk, mv)
        xp = xp + merge(gates, y_pool, y_swa, y_mem, w_branch_pool[l], w_branch_swa[l],
                        w_branch_mem[l], w_out[l])
        p_k.append(k.reshape(Bp, -1, SWA_KV_HEADS, SWA_HEAD_DIM)[:, -SWA_WINDOW:])
        p_v.append(v.reshape(Bp, -1, SWA_KV_HEADS, SWA_HEAD_DIM)[:, -SWA_WINDOW:])
        p_pool.append(pool_last)
        p_mk.append(mk)
        p_mv.append(mv)

        u, q, k, v, mq, gates = split_proj(rms_norm(xs, mix_norm[l]) @ w_in[l])
        y_pool, pool_new = pool_mixer(u, state_pool[l], pool_w[l], pool_scale[l])
        y_swa, k_win, v_win = swa_sample(q, k, v, cache_swa_k[l], cache_swa_v[l], swa_sinks[l], rel_bias)
        y_mem = mem_attend(mq, cache_mem_k[l], cache_mem_v[l])
        xs = xs + merge(gates, y_pool, y_swa, y_mem, w_branch_pool[l], w_branch_swa[l],
                        w_branch_mem[l], w_out[l])
        s_k.append(k_win)
        s_v.append(v_win)
        s_pool.append(pool_new)

        xp = swiglu_half(xp, ffn_b_norm[l], ffn_b_w_in[l], ffn_b_w_out[l])
        xs = swiglu_half(xs, ffn_b_norm[l], ffn_b_w_in[l], ffn_b_w_out[l])

    y_prompt = rms_norm(xp, final_norm)
    y_sample = rms_norm(xs, final_norm)
    return (y_prompt, y_sample, jnp.stack(p_k), jnp.stack(p_v), jnp.stack(p_pool),
            jnp.stack(p_mk), jnp.stack(p_mv), jnp.stack(s_k), jnp.stack(s_v), jnp.stack(s_pool))
```

```python
import functools
import math

import jax
import jax.numpy as jnp
from jax import lax
from jax.experimental import pallas as pl
from jax.experimental.pallas import tpu as pltpu

F32 = jnp.float32
BF16 = jnp.bfloat16

D_MODEL = 4096
D_FF = 11008
CHUNK = 64
EPS = 1e-6
POOL_WINDOWS = (2, 4, 8, 16)
POOL_GROUP = 256
POOL_WIDTH = 1024
POOL_HIST = 15
POOL_PAD = 16
SWA_WINDOW = 128
SWA_HEAD_DIM = 64
SWA_HEADS = 32
SWA_KV_HEADS = 4
SWA_GROUP = 8
SWA_WIDTH = 2048
SWA_KV_WIDTH = 256
SWA_SPAN = SWA_GROUP * SWA_HEAD_DIM
SWA_BAND = 256
N_MEM = 256
MEM_HEADS = 4
MEM_WIDTH = 1024
MEM_HEAD_DIM = 256
REL_BUCKETS = 32
REL_MAX_DIST = 128
NEG = -1e30

OFF_U = 0
OFF_Q = OFF_U + POOL_WIDTH
OFF_KV = OFF_Q + SWA_WIDTH
OFF_MQ = OFF_KV + 2 * SWA_KV_WIDTH
OFF_GATE = OFF_MQ + MEM_WIDTH

VMEM_LIMIT_BYTES = 52 * 1024 * 1024
BM = 544
BN = 512
BF = 256
BN_DOWN = 256


def _cparams(n_axes):
    return pltpu.CompilerParams(
        dimension_semantics=("arbitrary",) * n_axes,
        vmem_limit_bytes=VMEM_LIMIT_BYTES,
    )


def _rms_kernel(x_ref, g_ref, o_ref):
    x = x_ref[...]
    ms = jnp.mean(x * x, axis=-1, keepdims=True)
    o_ref[...] = (x * lax.rsqrt(ms + EPS) * g_ref[...]).astype(o_ref.dtype)


def _rmsnorm(x, g, out_dtype, bm=272):
    m, d = x.shape
    return pl.pallas_call(
        _rms_kernel,
        out_shape=jax.ShapeDtypeStruct((m, d), out_dtype),
        grid=(m // bm,),
        in_specs=[pl.BlockSpec((bm, d), lambda i: (i, 0)),
                  pl.BlockSpec((1, d), lambda i: (0, 0))],
        out_specs=pl.BlockSpec((bm, d), lambda i: (i, 0)),
        compiler_params=_cparams(1),
        name="rmsnorm",
    )(x, g.reshape(1, d))


def _proj_kernel(a_ref, w_ref, o_ref, wb_ref, *, scale):
    @pl.when(pl.program_id(1) == 0)
    def _():
        wb_ref[...] = w_ref[...].astype(BF16)

    acc = jnp.dot(a_ref[...], wb_ref[...], preferred_element_type=F32)
    if scale != 1.0:
        acc = acc * scale
    o_ref[...] = acc.astype(o_ref.dtype)


def _proj(a, w, col0, ncols, out_dtype, scale=1.0, bm=BM, bn=BN):
    m, k = a.shape
    off = col0 // bn
    return pl.pallas_call(
        functools.partial(_proj_kernel, scale=scale),
        out_shape=jax.ShapeDtypeStruct((m, ncols), out_dtype),
        grid=(ncols // bn, m // bm),
        in_specs=[pl.BlockSpec((bm, k), lambda j, i: (i, 0)),
                  pl.BlockSpec((k, bn), lambda j, i: (0, j + off))],
        out_specs=pl.BlockSpec((bm, bn), lambda j, i: (i, j)),
        scratch_shapes=[pltpu.VMEM((k, bn), BF16)],
        compiler_params=_cparams(2),
        name="proj",
    )(a, w)


def _proj_resid_kernel(a_ref, w_ref, r_ref, o_ref, wb_ref):
    @pl.when(pl.program_id(1) == 0)
    def _():
        wb_ref[...] = w_ref[...].astype(BF16)

    o_ref[...] = r_ref[...] + jnp.dot(a_ref[...], wb_ref[...], preferred_element_type=F32)


def _proj_resid(a, w, resid, bm=BM, bn=BN):
    m, k = a.shape
    n = w.shape[1]
    return pl.pallas_call(
        _proj_resid_kernel,
        out_shape=jax.ShapeDtypeStruct((m, n), F32),
        grid=(n // bn, m // bm),
        in_specs=[pl.BlockSpec((bm, k), lambda j, i: (i, 0)),
                  pl.BlockSpec((k, bn), lambda j, i: (0, j)),
                  pl.BlockSpec((bm, bn), lambda j, i: (i, j))],
        out_specs=pl.BlockSpec((bm, bn), lambda j, i: (i, j)),
        scratch_shapes=[pltpu.VMEM((k, bn), BF16)],
        compiler_params=_cparams(2),
        name="proj_resid",
    )(a, w, resid)


def _swiglu_kernel(a_ref, wg_ref, wu_ref, o_ref, wgb_ref, wub_ref):
    @pl.when(pl.program_id(1) == 0)
    def _():
        wgb_ref[...] = wg_ref[...].astype(BF16)
        wub_ref[...] = wu_ref[...].astype(BF16)

    a = a_ref[...]
    gate = jnp.dot(a, wgb_ref[...], preferred_element_type=F32)
    up = jnp.dot(a, wub_ref[...], preferred_element_type=F32)
    o_ref[...] = (gate * jax.nn.sigmoid(gate) * up).astype(o_ref.dtype)


def _ffn_up(a, w_in, bm=BM, bf=BF):
    m, k = a.shape
    nf = D_FF // bf
    return pl.pallas_call(
        _swiglu_kernel,
        out_shape=jax.ShapeDtypeStruct((m, D_FF), BF16),
        grid=(nf, m // bm),
        in_specs=[pl.BlockSpec((bm, k), lambda j, i: (i, 0)),
                  pl.BlockSpec((k, bf), lambda j, i: (0, j)),
                  pl.BlockSpec((k, bf), lambda j, i: (0, j + nf))],
        out_specs=pl.BlockSpec((bm, bf), lambda j, i: (i, j)),
        scratch_shapes=[pltpu.VMEM((k, bf), BF16), pltpu.VMEM((k, bf), BF16)],
        compiler_params=_cparams(2),
        name="ffn_up",
    )(a, w_in, w_in)


def _ffn_down_kernel(a_ref, w_ref, r_ref, o_ref):
    o_ref[...] = r_ref[...] + 0.5 * jnp.dot(a_ref[...], w_ref[...], preferred_element_type=F32)


def _ffn_down(a, w_bf16, resid, bm=BM, bn=BN_DOWN):
    m, k = a.shape
    n = w_bf16.shape[1]
    return pl.pallas_call(
        _ffn_down_kernel,
        out_shape=jax.ShapeDtypeStruct((m, n), F32),
        grid=(m // bm, n // bn),
        in_specs=[pl.BlockSpec((bm, k), lambda i, j: (i, 0)),
                  pl.BlockSpec((k, bn), lambda i, j: (0, j)),
                  pl.BlockSpec((bm, bn), lambda i, j: (i, j))],
        out_specs=pl.BlockSpec((bm, bn), lambda i, j: (i, j)),
        compiler_params=_cparams(2),
        name="ffn_down",
    )(a, w_bf16, resid)


def _merge_kernel(yp_ref, ya_ref, ym_ref, wp_ref, wa_ref, wm_ref, gp_ref, ga_ref, gm_ref,
                  o_ref, wpb_ref, wab_ref, wmb_ref):
    @pl.when(pl.program_id(1) == 0)
    def _():
        wpb_ref[...] = wp_ref[...].astype(BF16)
        wab_ref[...] = wa_ref[...].astype(BF16)
        wmb_ref[...] = wm_ref[...].astype(BF16)

    m = jax.nn.sigmoid(gp_ref[...]) * jnp.dot(yp_ref[...], wpb_ref[...], preferred_element_type=F32)
    m = m + jax.nn.sigmoid(ga_ref[...]) * jnp.dot(ya_ref[...], wab_ref[...], preferred_element_type=F32)
    m = m + jax.nn.sigmoid(gm_ref[...]) * jnp.dot(ym_ref[...], wmb_ref[...], preferred_element_type=F32)
    o_ref[...] = m.astype(o_ref.dtype)


def _merge(gates, y_pool, y_swa, y_mem, w_bp, w_bs, w_bm, bm=BM, bn=BN):
    m = gates.shape[0]
    nb = D_MODEL // bn
    act = lambda k: pl.BlockSpec((bm, k), lambda j, i: (i, 0))
    wgt = lambda k: pl.BlockSpec((k, bn), lambda j, i: (0, j))
    gate = lambda t: pl.BlockSpec((bm, bn), lambda j, i: (i, j + t * nb))
    return pl.pallas_call(
        _merge_kernel,
        out_shape=jax.ShapeDtypeStruct((m, D_MODEL), BF16),
        grid=(nb, m // bm),
        in_specs=[act(POOL_WIDTH), act(SWA_WIDTH), act(MEM_WIDTH),
                  wgt(POOL_WIDTH), wgt(SWA_WIDTH), wgt(MEM_WIDTH),
                  gate(0), gate(1), gate(2)],
        out_specs=pl.BlockSpec((bm, bn), lambda j, i: (i, j)),
        scratch_shapes=[pltpu.VMEM((POOL_WIDTH, bn), BF16), pltpu.VMEM((SWA_WIDTH, bn), BF16),
                        pltpu.VMEM((MEM_WIDTH, bn), BF16)],
        compiler_params=_cparams(2),
        name="merge",
    )(y_pool, y_swa, y_mem, w_bp, w_bs, w_bm, gates, gates, gates)


def _pool_kernel(u_ref, h_ref, w_ref, s_ref, o_ref, full_ref, *, n_hist, rc):
    bb, s_len, _ = u_ref.shape
    full_ref[:, 0:POOL_PAD, :] = h_ref[...]
    full_ref[:, POOL_PAD:, :] = u_ref[...]
    for g, win in enumerate(POOL_WINDOWS):
        lanes = slice(g * POOL_GROUP, (g + 1) * POOL_GROUP)
        wg = w_ref[g].astype(BF16)
        sc = s_ref[:, lanes]
        for r0 in range(0, s_len, rc):
            base = POOL_PAD + r0
            acc = full_ref[:, base:base + rc, lanes]
            cur = acc
            for d in range(1, win):
                acc = acc + full_ref[:, base - d:base - d + rc, lanes]
            t = lax.broadcasted_iota(jnp.int32, (bb, rc, POOL_GROUP), 1) + (r0 + n_hist + 1)
            cnt = jnp.minimum(t, win).astype(F32)
            mixed = (acc / cnt - cur).reshape(bb * rc, POOL_GROUP).astype(BF16)
            y = jnp.dot(mixed, wg, preferred_element_type=F32) * sc
            o_ref[:, r0:r0 + rc, lanes] = y.reshape(bb, rc, POOL_GROUP).astype(o_ref.dtype)


def _pool(u, hist, pool_w, pool_scale, n_hist, bb):
    nb, s_len, _ = u.shape
    rc = min(s_len, 256)
    return pl.pallas_call(
        functools.partial(_pool_kernel, n_hist=n_hist, rc=rc),
        out_shape=jax.ShapeDtypeStruct((nb, s_len, POOL_WIDTH), BF16),
        grid=(nb // bb,),
        in_specs=[pl.BlockSpec((bb, s_len, POOL_WIDTH), lambda b: (b, 0, 0)),
                  pl.BlockSpec((bb, POOL_PAD, POOL_WIDTH), lambda b: (b, 0, 0)),
                  pl.BlockSpec((len(POOL_WINDOWS), POOL_GROUP, POOL_GROUP), lambda b: (0, 0, 0)),
                  pl.BlockSpec((1, POOL_WIDTH), lambda b: (0, 0))],
        out_specs=pl.BlockSpec((bb, s_len, POOL_WIDTH), lambda b: (b, 0, 0)),
        scratch_shapes=[pltpu.VMEM((bb, POOL_PAD + s_len, POOL_WIDTH), F32)],
        compiler_params=_cparams(1),
        name="pool",
    )(u, hist, pool_w, pool_scale.reshape(1, POOL_WIDTH))


def _rel_bucket(rel):
    half = REL_BUCKETS // 2
    max_exact = half // 2
    n = jnp.abs(rel)
    nf = jnp.maximum(n, 1).astype(jnp.float32)
    large = max_exact + (jnp.log(nf / max_exact) / math.log(REL_MAX_DIST / max_exact)
                         * (half - max_exact)).astype(jnp.int32)
    large = jnp.minimum(large, half - 1)
    return jnp.where(rel > 0, half, 0) + jnp.where(n < max_exact, n, large)


def _bias_kernel(tab_ref, idx_ref, o_ref):
    h = pl.program_id(0)
    idx = idx_ref[...]
    acc = jnp.where(idx < 0, NEG, 0.0).astype(F32)
    for b in range(REL_BUCKETS):
        acc = acc + jnp.where(idx == b, tab_ref[b, h], 0.0)
    o_ref[0] = acc


def _bias_table(table, sq, n_keys, key_zero):
    band_lo = key_zero - SWA_WINDOW
    j = jnp.arange(SWA_BAND)[None, :]
    qi = jnp.arange(sq)[:, None]
    idx = _rel_bucket(j - key_zero - qi)
    idx = jnp.where((j >= band_lo) & (j < band_lo + n_keys), idx, -1).astype(jnp.int32)
    return pl.pallas_call(
        _bias_kernel,
        out_shape=jax.ShapeDtypeStruct((SWA_HEADS, sq, SWA_BAND), F32),
        grid=(SWA_HEADS,),
        in_specs=[pl.BlockSpec(memory_space=pltpu.SMEM),
                  pl.BlockSpec((sq, SWA_BAND), lambda h: (0, 0))],
        out_specs=pl.BlockSpec((1, sq, SWA_BAND), lambda h: (h, 0, 0)),
        compiler_params=_cparams(1),
        name="rel_bias",
    )(table, idx)


def _swa_kernel(*refs, n_kb, sq, masked):
    q_ref = refs[0]
    k_refs = refs[1:1 + n_kb]
    v_refs = refs[1 + n_kb:1 + 2 * n_kb]
    b_ref, s_ref, o_ref = refs[1 + 2 * n_kb:]
    rows = SWA_GROUP * sq
    if n_kb == 1:
        kband = k_refs[0][0]
        vband = v_refs[0][0]
    else:
        kband = jnp.concatenate([r[...] for r in k_refs], axis=0)
        vband = jnp.concatenate([r[...] for r in v_refs], axis=0)
    lane_grp = lax.shift_right_logical(lax.broadcasted_iota(jnp.int32, (sq, SWA_SPAN), 1), 6)
    if masked:
        lo = jnp.maximum(0, (3 - pl.program_id(1)) * CHUNK)
        col_ok = lax.broadcasted_iota(jnp.int32, (rows, SWA_BAND), 1) >= lo
    for h in range(SWA_KV_HEADS):
        lanes = slice(h * SWA_SPAN, (h + 1) * SWA_SPAN)
        qs = q_ref[:, lanes]
        qstack = jnp.concatenate(
            [jnp.where(lane_grp == g, qs, jnp.zeros_like(qs)) for g in range(SWA_GROUP)], axis=0)
        s = lax.dot_general(qstack, kband[:, lanes], (((1,), (1,)), ((), ())),
                            preferred_element_type=F32)
        s = s + b_ref[h * SWA_GROUP:(h + 1) * SWA_GROUP].reshape(rows, SWA_BAND)
        if masked:
            s = jnp.where(col_ok, s, NEG)
        sk = s_ref[h][:, 0:1]
        m = jnp.maximum(jnp.max(s, axis=1, keepdims=True), sk)
        e = jnp.exp(s - m)
        z = jnp.sum(e, axis=1, keepdims=True) + jnp.exp(sk - m)
        p = (e / z).astype(BF16)
        r = jnp.dot(p, vband[:, lanes], preferred_element_type=F32)
        o = jnp.where(lane_grp == 0, r[0:sq], 0.0)
        for g in range(1, SWA_GROUP):
            o = o + jnp.where(lane_grp == g, r[g * sq:(g + 1) * sq], 0.0)
        o_ref[:, lanes] = o.astype(o_ref.dtype)


def _sink_cols(sinks, sq):
    s = sinks.astype(F32).reshape(SWA_KV_HEADS, SWA_GROUP, 1, 1)
    return jnp.broadcast_to(s, (SWA_KV_HEADS, SWA_GROUP, sq, 128)).reshape(
        SWA_KV_HEADS, SWA_GROUP * sq, 128)


def _swa_prompt(q, kt, vt, bias, sinks, n_batch, n_chunk):
    sq = CHUNK
    n_kb = SWA_BAND // CHUNK
    qspec = pl.BlockSpec((sq, SWA_WIDTH), lambda b, c: (b * n_chunk + c, 0))

    def band(d):
        return pl.BlockSpec((CHUNK, SWA_WIDTH),
                            lambda b, c: (b * n_chunk + jnp.maximum(c - (n_kb - 1) + d, 0), 0))

    return pl.pallas_call(
        functools.partial(_swa_kernel, n_kb=n_kb, sq=sq, masked=True),
        out_shape=jax.ShapeDtypeStruct((n_batch * n_chunk * sq, SWA_WIDTH), BF16),
        grid=(n_batch, n_chunk),
        in_specs=[qspec] + [band(d) for d in range(n_kb)] * 2
        + [pl.BlockSpec((SWA_HEADS, sq, SWA_BAND), lambda b, c: (0, 0, 0)),
           pl.BlockSpec((SWA_KV_HEADS, SWA_GROUP * sq, 128), lambda b, c: (0, 0, 0))],
        out_specs=qspec,
        compiler_params=_cparams(2),
        name="swa_prompt",
    )(q, *([kt] * n_kb), *([vt] * n_kb), bias, _sink_cols(sinks, sq))


def _swa_sample(q, kt, vt, bias, sinks, row0, n_batch, sq):
    blk0 = row0 // sq
    kspec = pl.BlockSpec((1, SWA_BAND, SWA_WIDTH), lambda b, c: (b, 0, 0))
    return pl.pallas_call(
        functools.partial(_swa_kernel, n_kb=1, sq=sq, masked=False),
        out_shape=jax.ShapeDtypeStruct((n_batch * sq, SWA_WIDTH), BF16),
        grid=(n_batch, 1),
        in_specs=[pl.BlockSpec((sq, SWA_WIDTH), lambda b, c: (blk0 + b, 0)), kspec, kspec,
                  pl.BlockSpec((SWA_HEADS, sq, SWA_BAND), lambda b, c: (0, 0, 0)),
                  pl.BlockSpec((SWA_KV_HEADS, SWA_GROUP * sq, 128), lambda b, c: (0, 0, 0))],
        out_specs=pl.BlockSpec((sq, SWA_WIDTH), lambda b, c: (b, 0)),
        compiler_params=_cparams(2),
        name="swa_sample",
    )(q, kt, vt, bias, _sink_cols(sinks, sq))


def _mem_kernel(q_ref, k_ref, v_ref, o_ref):
    for h in range(MEM_HEADS):
        lanes = slice(h * MEM_HEAD_DIM, (h + 1) * MEM_HEAD_DIM)
        kh = k_ref[0, :, lanes].astype(BF16)
        vh = v_ref[0, :, lanes].astype(BF16)
        s = lax.dot_general(q_ref[:, lanes], kh, (((1,), (1,)), ((), ())),
                            preferred_element_type=F32)
        m = jnp.max(s, axis=1, keepdims=True)
        e = jnp.exp(s - m)
        p = (e / jnp.sum(e, axis=1, keepdims=True)).astype(BF16)
        o_ref[:, lanes] = jnp.dot(p, vh, preferred_element_type=F32).astype(o_ref.dtype)


def _mem_attend(mq, k, v, row0, n_batch, s_len, bq):
    per = s_len // bq
    blk0 = row0 // bq
    kspec = pl.BlockSpec((1, N_MEM, MEM_WIDTH), lambda b, s: (b, 0, 0))
    return pl.pallas_call(
        _mem_kernel,
        out_shape=jax.ShapeDtypeStruct((n_batch * s_len, MEM_WIDTH), BF16),
        grid=(n_batch, per),
        in_specs=[pl.BlockSpec((bq, MEM_WIDTH), lambda b, s: (blk0 + b * per + s, 0)), kspec, kspec],
        out_specs=pl.BlockSpec((bq, MEM_WIDTH), lambda b, s: (b * per + s, 0)),
        compiler_params=_cparams(2),
        name="mem_attend",
    )(mq, k, v)


def _tile_heads(t):
    r = t.shape[0]
    t = t.astype(BF16).reshape(r, SWA_KV_HEADS, 1, SWA_HEAD_DIM)
    return jnp.broadcast_to(t, (r, SWA_KV_HEADS, SWA_GROUP, SWA_HEAD_DIM)).reshape(r, SWA_WIDTH)


def _ffn_half(x, g, w_in, w_out):
    h = _rmsnorm(x, g, BF16)
    hh = _ffn_up(h, w_in)
    return _ffn_down(hh, w_out.astype(BF16), x)


def kernel(x_prompt, x_sample, cache_swa_k, cache_swa_v, state_pool, cache_mem_k, cache_mem_v,
           mem_prompt, ffn_a_norm, ffn_a_w_in, ffn_a_w_out, mix_norm, mem_norm, w_in, pool_w,
           pool_scale, swa_sinks, rel_bias, w_mem_kv, w_branch_pool, w_branch_swa, w_branch_mem,
           w_out, ffn_b_norm, ffn_b_w_in, ffn_b_w_out, final_norm):
    bp, sp, _ = x_prompt.shape
    bs, ss, _ = x_sample.shape
    depth = w_in.shape[0]
    tp = bp * sp
    ts = bs * ss
    n_chunk = sp // CHUNK
    n_cache = cache_swa_k.shape[2]

    x = jnp.concatenate([x_prompt.reshape(tp, D_MODEL), x_sample.reshape(ts, D_MODEL)], axis=0)
    mem_x = mem_prompt.reshape(bp * N_MEM, D_MODEL)

    bias_p = _bias_table(rel_bias, CHUNK, SWA_BAND - CHUNK, SWA_BAND - CHUNK)
    bias_s = _bias_table(rel_bias, ss, n_cache + ss, n_cache)
    hist_p = jnp.zeros((bp, POOL_PAD, POOL_WIDTH), F32)

    p_k, p_v, p_pool, p_mk, p_mv, s_k, s_v, s_pool = ([] for _ in range(8))
    for l in range(depth):
        x = _ffn_half(x, ffn_a_norm[l], ffn_a_w_in[l], ffn_a_w_out[l])

        h = _rmsnorm(x, mix_norm[l], BF16)
        wl = w_in[l]
        u = _proj(h, wl, OFF_U, POOL_WIDTH, F32)
        q = _proj(h, wl, OFF_Q, SWA_WIDTH, BF16, scale=SWA_HEAD_DIM ** -0.5)
        kv = _proj(h, wl, OFF_KV, 2 * SWA_KV_WIDTH, F32)
        mq = _proj(h, wl, OFF_MQ, MEM_WIDTH, BF16, scale=MEM_HEAD_DIM ** -0.5)
        gates = _proj(h, wl, OFF_GATE, 3 * D_MODEL, F32)

        u_p = u[:tp].reshape(bp, sp, POOL_WIDTH)
        u_s = u[tp:].reshape(bs, ss, POOL_WIDTH)
        hist_s = jnp.pad(state_pool[l], ((0, 0), (POOL_PAD - POOL_HIST, 0), (0, 0)))
        y_pool = jnp.concatenate([
            _pool(u_p, hist_p, pool_w[l], pool_scale[l], 0, 1).reshape(tp, POOL_WIDTH),
            _pool(u_s, hist_s, pool_w[l], pool_scale[l], POOL_HIST, bs).reshape(ts, POOL_WIDTH)], axis=0)
        p_pool.append(u_p[:, sp - POOL_HIST:])
        s_pool.append(jnp.concatenate([state_pool[l], u_s], axis=1)[:, ss:])

        k_new, v_new = kv[:, :SWA_KV_WIDTH], kv[:, SWA_KV_WIDTH:]
        kt, vt = _tile_heads(k_new), _tile_heads(v_new)
        y_swa_p = _swa_prompt(q, kt, vt, bias_p, swa_sinks[l], bp, n_chunk)
        kk = jnp.concatenate([cache_swa_k[l].reshape(bs, n_cache, SWA_KV_WIDTH),
                              k_new[tp:].reshape(bs, ss, SWA_KV_WIDTH)], axis=1)
        vv = jnp.concatenate([cache_swa_v[l].reshape(bs, n_cache, SWA_KV_WIDTH),
                              v_new[tp:].reshape(bs, ss, SWA_KV_WIDTH)], axis=1)
        pad = ((0, 0), (0, SWA_BAND - n_cache - ss), (0, 0))
        kt_s = _tile_heads(jnp.pad(kk, pad).reshape(bs * SWA_BAND, SWA_KV_WIDTH)).reshape(
            bs, SWA_BAND, SWA_WIDTH)
        vt_s = _tile_heads(jnp.pad(vv, pad).reshape(bs * SWA_BAND, SWA_KV_WIDTH)).reshape(
            bs, SWA_BAND, SWA_WIDTH)
        y_swa_s = _swa_sample(q, kt_s, vt_s, bias_s, swa_sinks[l], tp, bs, ss)
        y_swa = jnp.concatenate([y_swa_p, y_swa_s], axis=0)
        kvh = (SWA_KV_HEADS, SWA_HEAD_DIM)
        p_k.append(k_new[:tp].reshape(bp, sp, *kvh)[:, sp - SWA_WINDOW:])
        p_v.append(v_new[:tp].reshape(bp, sp, *kvh)[:, sp - SWA_WINDOW:])
        s_k.append(kk[:, ss:].reshape(bs, n_cache, *kvh))
        s_v.append(vv[:, ss:].reshape(bs, n_cache, *kvh))

        mkv = _proj(_rmsnorm(mem_x, mem_norm[l], BF16, bm=256), w_mem_kv[l], 0, 2 * MEM_WIDTH, F32,
                    bm=512)
        mk = mkv[:, :MEM_WIDTH].reshape(bp, N_MEM, MEM_WIDTH)
        mv = mkv[:, MEM_WIDTH:].reshape(bp, N_MEM, MEM_WIDTH)
        y_mem = jnp.concatenate([
            _mem_attend(mq, mk, mv, 0, bp, sp, 512),
            _mem_attend(mq, cache_mem_k[l].reshape(bs, N_MEM, MEM_WIDTH),
                        cache_mem_v[l].reshape(bs, N_MEM, MEM_WIDTH), tp, bs, ss, ss)], axis=0)
        p_mk.append(mk.reshape(bp, N_MEM, MEM_HEADS, MEM_HEAD_DIM))
        p_mv.append(mv.reshape(bp, N_MEM, MEM_HEADS, MEM_HEAD_DIM))

        m = _merge(gates, y_pool, y_swa, y_mem, w_branch_pool[l], w_branch_swa[l], w_branch_mem[l])
        x = _proj_resid(m, w_out[l], x)

        x = _ffn_half(x, ffn_b_norm[l], ffn_b_w_in[l], ffn_b_w_out[l])

    y = _rmsnorm(x, final_norm, F32)
    return (y[:tp].reshape(bp, sp, D_MODEL), y[tp:].reshape(bs, ss, D_MODEL),
            jnp.stack(p_k), jnp.stack(p_v), jnp.stack(p_pool), jnp.stack(p_mk), jnp.stack(p_mv),
            jnp.stack(s_k), jnp.stack(s_v), jnp.stack(s_pool))
```

```python
import functools
import math

import jax
import jax.numpy as jnp
from jax import lax
from jax.experimental import pallas as pl
from jax.experimental.pallas import tpu as pltpu

F32 = jnp.float32
BF16 = jnp.bfloat16

D_MODEL = 4096
D_FF = 11008
CHUNK = 64
EPS = 1e-6
POOL_WINDOWS = (2, 4, 8, 16)
POOL_GROUP = 256
POOL_WIDTH = 1024
POOL_HIST = 15
POOL_PAD = 16
SWA_WINDOW = 128
SWA_HEAD_DIM = 64
SWA_HEADS = 32
SWA_KV_HEADS = 4
SWA_GROUP = 8
SWA_WIDTH = 2048
SWA_KV_WIDTH = 256
SWA_SPAN = SWA_GROUP * SWA_HEAD_DIM
SWA_BAND = 256
N_MEM = 256
MEM_HEADS = 4
MEM_WIDTH = 1024
MEM_HEAD_DIM = 256
REL_BUCKETS = 32
REL_MAX_DIST = 128
NEG = -1e30

OFF_U = 0
OFF_Q = OFF_U + POOL_WIDTH
OFF_KV = OFF_Q + SWA_WIDTH
OFF_MQ = OFF_KV + 2 * SWA_KV_WIDTH
OFF_GATE = OFF_MQ + MEM_WIDTH

VMEM_LIMIT_BYTES = 58 * 1024 * 1024
BM = 272
WB = 512
NW = 2
BM_MERGE = 544
FB = 256
F_MAIN = 42 * FB
BM_DOWN = 544
BN_DOWN = 256


def _cparams(n_axes):
    return pltpu.CompilerParams(
        dimension_semantics=("arbitrary",) * n_axes,
        vmem_limit_bytes=VMEM_LIMIT_BYTES,
    )


def _any_spec():
    return pl.BlockSpec(memory_space=pl.ANY)


def _rms_kernel(x_ref, g_ref, o_ref):
    x = x_ref[...]
    ms = jnp.mean(x * x, axis=-1, keepdims=True)
    o_ref[...] = (x * lax.rsqrt(ms + EPS) * g_ref[...]).astype(o_ref.dtype)


def _rmsnorm(x, g, out_dtype, bm=272, row0=0, rows=None):
    d = x.shape[1]
    rows = x.shape[0] if rows is None else rows
    blk0 = row0 // bm
    g_arr, l = g
    if l is None:
        gspec = pl.BlockSpec((1, d), lambda i: (0, 0))
    else:
        gspec = pl.BlockSpec((None, 1, d), lambda i: (l, 0, 0))
    return pl.pallas_call(
        _rms_kernel,
        out_shape=jax.ShapeDtypeStruct((rows, d), out_dtype),
        grid=(rows // bm,),
        in_specs=[pl.BlockSpec((bm, d), lambda i: (blk0 + i, 0)), gspec],
        out_specs=pl.BlockSpec((bm, d), lambda i: (i, 0)),
        compiler_params=_cparams(1),
        name="rmsnorm",
    )(x, g_arr)


def _stage_weights(w_refs, wb_ref):
    @pl.when(pl.program_id(1) == 0)
    def _():
        wd = w_refs[0].shape[1]
        for t, w in enumerate(w_refs):
            wb_ref[:, t * wd:(t + 1) * wd] = w[...].astype(BF16)


def _wspecs(k, wb, l, first, nw):
    return [pl.BlockSpec((None, k, wb), functools.partial(lambda j, i, t: (l, 0, first + nw * j + t), t=t))
            for t in range(nw)]


def _proj_kernel(*refs, nw, scale):
    a_ref, w_refs, o_ref, wb_ref = refs[0], refs[1:1 + nw], refs[1 + nw], refs[2 + nw]
    _stage_weights(w_refs, wb_ref)
    acc = jnp.dot(a_ref[...], wb_ref[...], preferred_element_type=F32)
    if scale != 1.0:
        acc = acc * scale
    o_ref[...] = acc.astype(o_ref.dtype)


def _proj(a, w, l, col0, ncols, out_dtype, scale=1.0, bm=BM, wb=WB, nw=NW):
    m, k = a.shape
    bn = wb * nw
    return pl.pallas_call(
        functools.partial(_proj_kernel, nw=nw, scale=scale),
        out_shape=jax.ShapeDtypeStruct((m, ncols), out_dtype),
        grid=(ncols // bn, m // bm),
        in_specs=[pl.BlockSpec((bm, k), lambda j, i: (i, 0))] + _wspecs(k, wb, l, col0 // wb, nw),
        out_specs=pl.BlockSpec((bm, bn), lambda j, i: (i, j)),
        scratch_shapes=[pltpu.VMEM((k, bn), BF16)],
        compiler_params=_cparams(2),
        name="proj",
    )(a, *([w] * nw))


def _proj_kv_kernel(a_ref, w_ref, e_ref, kv_ref, kt_ref, vt_ref, wb_ref):
    _stage_weights((w_ref,), wb_ref)
    acc = jnp.dot(a_ref[...], wb_ref[...], preferred_element_type=F32)
    kv_ref[...] = acc
    kv = acc.astype(BF16)
    kt_ref[...] = jnp.dot(kv[:, :SWA_KV_WIDTH], e_ref[...], preferred_element_type=F32).astype(BF16)
    vt_ref[...] = jnp.dot(kv[:, SWA_KV_WIDTH:], e_ref[...], preferred_element_type=F32).astype(BF16)


def _proj_kv(a, w, l, expand, bm=BM):
    m, k = a.shape
    n = 2 * SWA_KV_WIDTH
    row = lambda width: pl.BlockSpec((bm, width), lambda j, i: (i, 0))
    return pl.pallas_call(
        _proj_kv_kernel,
        out_shape=(jax.ShapeDtypeStruct((m, n), F32),
                   jax.ShapeDtypeStruct((m, SWA_WIDTH), BF16),
                   jax.ShapeDtypeStruct((m, SWA_WIDTH), BF16)),
        grid=(1, m // bm),
        in_specs=[row(k)] + _wspecs(k, n, l, OFF_KV // n, 1)
        + [pl.BlockSpec((SWA_KV_WIDTH, SWA_WIDTH), lambda j, i: (0, 0))],
        out_specs=(row(n), row(SWA_WIDTH), row(SWA_WIDTH)),
        scratch_shapes=[pltpu.VMEM((k, n), BF16)],
        compiler_params=_cparams(2),
        name="proj_kv",
    )(a, w, expand)


def _proj_resid_kernel(*refs, nw):
    a_ref, w_refs = refs[0], refs[1:1 + nw]
    r_ref, o_ref, wb_ref = refs[1 + nw:]
    _stage_weights(w_refs, wb_ref)
    o_ref[...] = r_ref[...] + jnp.dot(a_ref[...], wb_ref[...], preferred_element_type=F32)


def _proj_resid(a, w, l, resid, bm=BM, wb=WB, nw=NW):
    m, k = a.shape
    n = w.shape[2]
    bn = wb * nw
    tile = pl.BlockSpec((bm, bn), lambda j, i: (i, j))
    return pl.pallas_call(
        functools.partial(_proj_resid_kernel, nw=nw),
        out_shape=jax.ShapeDtypeStruct((m, n), F32),
        grid=(n // bn, m // bm),
        in_specs=[pl.BlockSpec((bm, k), lambda j, i: (i, 0))] + _wspecs(k, wb, l, 0, nw) + [tile],
        out_specs=tile,
        scratch_shapes=[pltpu.VMEM((k, bn), BF16)],
        compiler_params=_cparams(2),
        name="proj_resid",
    )(a, *([w] * nw), resid)


def _swiglu_kernel(*refs, nw):
    a_ref, w_refs, o_ref, wb_ref = refs[0], refs[1:1 + nw], refs[1 + nw], refs[2 + nw]
    _stage_weights(w_refs, wb_ref)
    acc = jnp.dot(a_ref[...], wb_ref[...], preferred_element_type=F32)
    half = acc.shape[1] // 2
    gate, up = acc[:, :half], acc[:, half:]
    o_ref[...] = (gate * jax.nn.sigmoid(gate) * up).astype(o_ref.dtype)


def _ffn_up(a, w_in, l, first, npair, ncols, bm=BM, fb=FB):
    m, k = a.shape
    nf = D_FF // fb
    nw = 2 * npair
    gate = [pl.BlockSpec((None, k, fb), functools.partial(lambda j, i, t: (l, 0, first + npair * j + t), t=t))
            for t in range(npair)]
    up = [pl.BlockSpec((None, k, fb), functools.partial(lambda j, i, t: (l, 0, nf + first + npair * j + t), t=t))
          for t in range(npair)]
    bn = npair * fb
    return pl.pallas_call(
        functools.partial(_swiglu_kernel, nw=nw),
        out_shape=jax.ShapeDtypeStruct((m, ncols), BF16),
        grid=(ncols // bn, m // bm),
        in_specs=[pl.BlockSpec((bm, k), lambda j, i: (i, 0))] + gate + up,
        out_specs=pl.BlockSpec((bm, bn), lambda j, i: (i, j)),
        scratch_shapes=[pltpu.VMEM((k, nw * fb), BF16)],
        compiler_params=_cparams(2),
        name="ffn_up",
    )(a, *([w_in] * nw))


def _cast_kernel(w_ref, o_ref):
    o_ref[...] = w_ref[...].astype(o_ref.dtype)


def _cast_bf16(w, l, br=688, bc=2048):
    _, r, c = w.shape
    return pl.pallas_call(
        _cast_kernel,
        out_shape=jax.ShapeDtypeStruct((r, c), BF16),
        grid=(r // br, c // bc),
        in_specs=[pl.BlockSpec((None, br, bc), lambda i, j: (l, i, j))],
        out_specs=pl.BlockSpec((br, bc), lambda i, j: (i, j)),
        compiler_params=_cparams(2),
        name="cast_bf16",
    )(w)


def _ffn_down_kernel(a1_ref, a2_ref, w_ref, r_ref, o_ref):
    k1 = a1_ref.shape[1]
    acc = jnp.dot(a1_ref[...], w_ref[0:k1, :], preferred_element_type=F32)
    acc = acc + jnp.dot(a2_ref[...], w_ref[k1:, :], preferred_element_type=F32)
    o_ref[...] = r_ref[...] + 0.5 * acc


def _ffn_down(a1, a2, w_bf16, resid, bm=BM_DOWN, bn=BN_DOWN):
    m, k1 = a1.shape
    k2 = a2.shape[1]
    n = w_bf16.shape[1]
    tile = pl.BlockSpec((bm, bn), lambda i, j: (i, j))
    return pl.pallas_call(
        _ffn_down_kernel,
        out_shape=jax.ShapeDtypeStruct((m, n), F32),
        grid=(m // bm, n // bn),
        in_specs=[pl.BlockSpec((bm, k1), lambda i, j: (i, 0)),
                  pl.BlockSpec((bm, k2), lambda i, j: (i, 0)),
                  pl.BlockSpec((k1 + k2, bn), lambda i, j: (0, j)), tile],
        out_specs=tile,
        compiler_params=_cparams(2),
        name="ffn_down",
    )(a1, a2, w_bf16, resid)


def _merge_kernel(yp_ref, ya_ref, ym_ref, wp_ref, wa_ref, wm_ref, gp_ref, ga_ref, gm_ref,
                  o_ref, wpb_ref, wab_ref, wmb_ref):
    @pl.when(pl.program_id(1) == 0)
    def _():
        wpb_ref[...] = wp_ref[...].astype(BF16)
        wab_ref[...] = wa_ref[...].astype(BF16)
        wmb_ref[...] = wm_ref[...].astype(BF16)

    m = jax.nn.sigmoid(gp_ref[...]) * jnp.dot(yp_ref[...], wpb_ref[...], preferred_element_type=F32)
    m = m + jax.nn.sigmoid(ga_ref[...]) * jnp.dot(ya_ref[...], wab_ref[...], preferred_element_type=F32)
    m = m + jax.nn.sigmoid(gm_ref[...]) * jnp.dot(ym_ref[...], wmb_ref[...], preferred_element_type=F32)
    o_ref[...] = m.astype(o_ref.dtype)


def _merge(gates, y_pool, y_swa, y_mem, w_bp, w_bs, w_bm, l, bm=BM_MERGE, bn=WB):
    m = gates.shape[0]
    nb = D_MODEL // bn
    act = lambda k: pl.BlockSpec((bm, k), lambda j, i: (i, 0))
    wgt = lambda k: pl.BlockSpec((None, k, bn), lambda j, i: (l, 0, j))
    gate = lambda t: pl.BlockSpec((bm, bn), lambda j, i: (i, j + t * nb))
    return pl.pallas_call(
        _merge_kernel,
        out_shape=jax.ShapeDtypeStruct((m, D_MODEL), BF16),
        grid=(nb, m // bm),
        in_specs=[act(POOL_WIDTH), act(SWA_WIDTH), act(MEM_WIDTH),
                  wgt(POOL_WIDTH), wgt(SWA_WIDTH), wgt(MEM_WIDTH),
                  gate(0), gate(1), gate(2)],
        out_specs=pl.BlockSpec((bm, bn), lambda j, i: (i, j)),
        scratch_shapes=[pltpu.VMEM((POOL_WIDTH, bn), BF16), pltpu.VMEM((SWA_WIDTH, bn), BF16),
                        pltpu.VMEM((MEM_WIDTH, bn), BF16)],
        compiler_params=_cparams(2),
        name="merge",
    )(y_pool, y_swa, y_mem, w_bp, w_bs, w_bm, gates, gates, gates)


def _pool_kernel(*refs, bb, s_len, n_hist, rc):
    u_ref, h_ref, w_ref, s_ref = refs[:4]
    o_ref, full_ref = refs[-2:]
    full_ref[:, 0:POOL_PAD, :] = h_ref[...]
    full_ref[:, POOL_PAD:, :] = u_ref[...].reshape(bb, s_len, POOL_WIDTH)
    for g, win in enumerate(POOL_WINDOWS):
        lanes = slice(g * POOL_GROUP, (g + 1) * POOL_GROUP)
        wg = w_ref[g].astype(BF16)
        sc = s_ref[:, lanes]
        for r0 in range(0, s_len, rc):
            base = POOL_PAD + r0
            cur = full_ref[:, base:base + rc, lanes]
            acc = cur
            for d in range(1, win):
                acc = acc + full_ref[:, base - d:base - d + rc, lanes]
            t = lax.broadcasted_iota(jnp.int32, (bb, rc, POOL_GROUP), 1) + (r0 + n_hist + 1)
            cnt = jnp.minimum(t, win).astype(F32)
            mixed = (acc / cnt - cur).reshape(bb * rc, POOL_GROUP).astype(BF16)
            y = jnp.dot(mixed, wg, preferred_element_type=F32) * sc
            o_ref[r0:r0 + bb * rc, lanes] = y.astype(o_ref.dtype)


def _pool(u, hist, pool_w, pool_scale, l, n_hist, n_batch, s_len, bb, row0, out_rows, prev=None):
    rows = bb * s_len
    blk0 = row0 // rows
    rc = min(s_len, 256)
    assert bb == 1 or rc == s_len
    if hist.ndim == 4:
        hspec = pl.BlockSpec((None, bb, POOL_PAD, POOL_WIDTH), lambda b: (l, b, 0, 0))
    else:
        hspec = pl.BlockSpec((bb, POOL_PAD, POOL_WIDTH), lambda b: (b, 0, 0))
    in_specs = [pl.BlockSpec((rows, POOL_WIDTH), lambda b: (blk0 + b, 0)), hspec,
                pl.BlockSpec((None, len(POOL_WINDOWS), POOL_GROUP, POOL_GROUP), lambda b: (l, 0, 0, 0)),
                pl.BlockSpec((None, 1, POOL_WIDTH), lambda b: (l, 0, 0))]
    args = [u, hist, pool_w, pool_scale]
    aliases = {}
    if prev is not None:
        in_specs.append(_any_spec())
        args.append(prev)
        aliases = {4: 0}
    return pl.pallas_call(
        functools.partial(_pool_kernel, bb=bb, s_len=s_len, n_hist=n_hist, rc=rc),
        out_shape=jax.ShapeDtypeStruct((out_rows, POOL_WIDTH), BF16),
        grid=(n_batch // bb,),
        in_specs=in_specs,
        out_specs=pl.BlockSpec((rows, POOL_WIDTH), lambda b: (blk0 + b, 0)),
        scratch_shapes=[pltpu.VMEM((bb, POOL_PAD + s_len, POOL_WIDTH), F32)],
        input_output_aliases=aliases,
        compiler_params=_cparams(1),
        name="pool",
    )(*args)


def _rel_bucket(rel):
    half = REL_BUCKETS // 2
    max_exact = half // 2
    n = jnp.abs(rel)
    nf = jnp.maximum(n, 1).astype(jnp.float32)
    large = max_exact + (jnp.log(nf / max_exact) / math.log(REL_MAX_DIST / max_exact)
                         * (half - max_exact)).astype(jnp.int32)
    large = jnp.minimum(large, half - 1)
    return jnp.where(rel > 0, half, 0) + jnp.where(n < max_exact, n, large)


def _bias_kernel(tab_ref, idx_ref, o_ref):
    h = pl.program_id(0)
    idx = idx_ref[...]
    acc = jnp.where(idx < 0, NEG, 0.0).astype(F32)
    for b in range(REL_BUCKETS):
        acc = acc + jnp.where(idx == b, tab_ref[b, h], 0.0)
    o_ref[0] = acc


def _bias_table(table, sq, n_keys, key_zero):
    band_lo = key_zero - SWA_WINDOW
    j = jnp.arange(SWA_BAND)[None, :]
    qi = jnp.arange(sq)[:, None]
    idx = _rel_bucket(j - key_zero - qi)
    idx = jnp.where((j >= band_lo) & (j < band_lo + n_keys), idx, -1).astype(jnp.int32)
    return pl.pallas_call(
        _bias_kernel,
        out_shape=jax.ShapeDtypeStruct((SWA_HEADS, sq, SWA_BAND), F32),
        grid=(SWA_HEADS,),
        in_specs=[pl.BlockSpec(memory_space=pltpu.SMEM),
                  pl.BlockSpec((sq, SWA_BAND), lambda h: (0, 0))],
        out_specs=pl.BlockSpec((1, sq, SWA_BAND), lambda h: (h, 0, 0)),
        compiler_params=_cparams(1),
        name="rel_bias",
    )(table, idx)


def _swa_core(q, kband, vband, bias, sink_col, col_ok, sq):
    rows = SWA_GROUP * sq
    lane_grp = lax.shift_right_logical(lax.broadcasted_iota(jnp.int32, (sq, SWA_SPAN), 1), 6)
    parts = []
    for h in range(SWA_KV_HEADS):
        lanes = slice(h * SWA_SPAN, (h + 1) * SWA_SPAN)
        qs = q[:, lanes]
        qstack = jnp.concatenate(
            [jnp.where(lane_grp == g, qs, jnp.zeros_like(qs)) for g in range(SWA_GROUP)], axis=0)
        parts.append(lax.dot_general(qstack, kband[:, lanes], (((1,), (1,)), ((), ())),
                                     preferred_element_type=F32))
    s = jnp.concatenate(parts, axis=0) + bias
    if col_ok is not None:
        s = jnp.where(col_ok, s, NEG)
    m = jnp.maximum(jnp.max(s, axis=1, keepdims=True), sink_col)
    e = jnp.exp(s - m)
    z = jnp.sum(e, axis=1, keepdims=True) + jnp.exp(sink_col - m)
    p = (e / z).astype(BF16)
    outs = []
    for h in range(SWA_KV_HEADS):
        lanes = slice(h * SWA_SPAN, (h + 1) * SWA_SPAN)
        r = jnp.dot(p[h * rows:(h + 1) * rows], vband[:, lanes], preferred_element_type=F32)
        o = jnp.where(lane_grp == 0, r[0:sq], 0.0)
        for g in range(1, SWA_GROUP):
            o = o + jnp.where(lane_grp == g, r[g * sq:(g + 1) * sq], 0.0)
        outs.append(o)
    return jnp.concatenate(outs, axis=1)


def _swa_prompt_kernel(*refs, n_kb):
    q_ref = refs[0]
    k_refs = refs[1:1 + n_kb]
    v_refs = refs[1 + n_kb:1 + 2 * n_kb]
    b_ref, s_ref, o_ref = refs[1 + 2 * n_kb:]
    sq = q_ref.shape[0]
    kband = jnp.concatenate([r[...] for r in k_refs], axis=0)
    vband = jnp.concatenate([r[...] for r in v_refs], axis=0)
    lo = jnp.maximum(0, (n_kb - 1 - pl.program_id(1)) * CHUNK)
    col_ok = lax.broadcasted_iota(jnp.int32, (SWA_HEADS * sq, SWA_BAND), 1) >= lo
    bias = b_ref[...].reshape(SWA_HEADS * sq, SWA_BAND)
    o = _swa_core(q_ref[...], kband, vband, bias, s_ref[:, 0:1], col_ok, sq)
    o_ref[...] = o.astype(o_ref.dtype)


def _swa_prompt(q, kt, vt, bias, sink_cols, l, n_batch, n_chunk):
    sq = CHUNK
    n_kb = SWA_BAND // CHUNK
    qspec = pl.BlockSpec((sq, SWA_WIDTH), lambda b, c: (b * n_chunk + c, 0))

    def band(d):
        return pl.BlockSpec((CHUNK, SWA_WIDTH),
                            lambda b, c: (b * n_chunk + jnp.maximum(c - (n_kb - 1) + d, 0), 0))

    return pl.pallas_call(
        functools.partial(_swa_prompt_kernel, n_kb=n_kb),
        out_shape=jax.ShapeDtypeStruct(q.shape, BF16),
        grid=(n_batch, n_chunk),
        in_specs=[qspec] + [band(d) for d in range(n_kb)] * 2
        + [pl.BlockSpec((SWA_HEADS, sq, SWA_BAND), lambda b, c: (0, 0, 0)),
           pl.BlockSpec((None, SWA_HEADS * sq, 128), lambda b, c: (l, 0, 0))],
        out_specs=qspec,
        compiler_params=_cparams(2),
        name="swa_prompt",
    )(q, *([kt] * n_kb), *([vt] * n_kb), bias, sink_cols)


def _swa_sample_kernel(q_ref, ck_ref, cv_ref, kvn_ref, e_ref, b_ref, s_ref, prev_ref, o_ref):
    del prev_ref
    sq = q_ref.shape[0]
    n_cache = ck_ref.shape[0]
    kvn = kvn_ref[...]
    zpad = jnp.zeros((SWA_BAND - n_cache - sq, SWA_KV_WIDTH), F32)
    kk = jnp.concatenate([ck_ref[...], kvn[:, :SWA_KV_WIDTH], zpad], axis=0).astype(BF16)
    vv = jnp.concatenate([cv_ref[...], kvn[:, SWA_KV_WIDTH:], zpad], axis=0).astype(BF16)
    kband = jnp.dot(kk, e_ref[...], preferred_element_type=F32).astype(BF16)
    vband = jnp.dot(vv, e_ref[...], preferred_element_type=F32).astype(BF16)
    bias = b_ref[...].reshape(SWA_HEADS * sq, SWA_BAND)
    o = _swa_core(q_ref[...], kband, vband, bias, s_ref[:, 0:1], None, sq)
    o_ref[...] = o.astype(o_ref.dtype)


def _swa_sample(q, cache_k, cache_v, kv, expand, bias, sink_cols, l, row0, n_batch, sq, prev):
    blk0 = row0 // sq
    n_cache = cache_k.shape[2]
    cspec = pl.BlockSpec((None, None, n_cache, SWA_KV_WIDTH), lambda b: (l, b, 0, 0))
    return pl.pallas_call(
        _swa_sample_kernel,
        out_shape=jax.ShapeDtypeStruct(prev.shape, BF16),
        grid=(n_batch,),
        in_specs=[pl.BlockSpec((sq, SWA_WIDTH), lambda b: (blk0 + b, 0)), cspec, cspec,
                  pl.BlockSpec((sq, 2 * SWA_KV_WIDTH), lambda b: (blk0 + b, 0)),
                  pl.BlockSpec((SWA_KV_WIDTH, SWA_WIDTH), lambda b: (0, 0)),
                  pl.BlockSpec((SWA_HEADS, sq, SWA_BAND), lambda b: (0, 0, 0)),
                  pl.BlockSpec((None, SWA_HEADS * sq, 128), lambda b: (l, 0, 0)),
                  _any_spec()],
        out_specs=pl.BlockSpec((sq, SWA_WIDTH), lambda b: (blk0 + b, 0)),
        input_output_aliases={7: 0},
        compiler_params=_cparams(1),
        name="swa_sample",
    )(q, cache_k, cache_v, kv, expand, bias, sink_cols, prev)


def _mem_kernel(*refs):
    q_ref, k_ref, v_ref = refs[:3]
    o_ref = refs[-1]
    for h in range(MEM_HEADS):
        lanes = slice(h * MEM_HEAD_DIM, (h + 1) * MEM_HEAD_DIM)
        kh = k_ref[:, lanes].astype(BF16)
        vh = v_ref[:, lanes].astype(BF16)
        s = lax.dot_general(q_ref[:, lanes], kh, (((1,), (1,)), ((), ())),
                            preferred_element_type=F32)
        m = jnp.max(s, axis=1, keepdims=True)
        e = jnp.exp(s - m)
        p = (e / jnp.sum(e, axis=1, keepdims=True)).astype(BF16)
        o_ref[:, lanes] = jnp.dot(p, vh, preferred_element_type=F32).astype(o_ref.dtype)


def _mem_attend(mq, kspec, vspec, k, v, row0, n_batch, s_len, bq, prev=None):
    per = s_len // bq
    blk0 = row0 // bq
    qspec = pl.BlockSpec((bq, MEM_WIDTH), lambda b, s: (blk0 + b * per + s, 0))
    in_specs = [qspec, kspec, vspec]
    args = [mq, k, v]
    aliases = {}
    if prev is not None:
        in_specs.append(_any_spec())
        args.append(prev)
        aliases = {3: 0}
    return pl.pallas_call(
        _mem_kernel,
        out_shape=jax.ShapeDtypeStruct(mq.shape, BF16),
        grid=(n_batch, per),
        in_specs=in_specs,
        out_specs=qspec,
        input_output_aliases=aliases,
        compiler_params=_cparams(2),
        name="mem_attend",
    )(*args)


def _head_expansion():
    src = jnp.arange(SWA_KV_WIDTH)
    dst = jnp.arange(SWA_WIDTH)
    dst_src = (dst // SWA_SPAN) * SWA_HEAD_DIM + dst % SWA_HEAD_DIM
    return (src[:, None] == dst_src[None, :]).astype(BF16)


def _ffn_half(x, g, w_in, w_out, l):
    h = _rmsnorm(x, (g, l), BF16)
    hh_main = _ffn_up(h, w_in, l, 0, 2, F_MAIN)
    hh_tail = _ffn_up(h, w_in, l, F_MAIN // FB, 1, D_FF - F_MAIN)
    return _ffn_down(hh_main, hh_tail, _cast_bf16(w_out, l), x)


def kernel(x_prompt, x_sample, cache_swa_k, cache_swa_v, state_pool, cache_mem_k, cache_mem_v,
           mem_prompt, ffn_a_norm, ffn_a_w_in, ffn_a_w_out, mix_norm, mem_norm, w_in, pool_w,
           pool_scale, swa_sinks, rel_bias, w_mem_kv, w_branch_pool, w_branch_swa, w_branch_mem,
           w_out, ffn_b_norm, ffn_b_w_in, ffn_b_w_out, final_norm):
    bp, sp, _ = x_prompt.shape
    bs, ss, _ = x_sample.shape
    depth = w_in.shape[0]
    tp = bp * sp
    ts = bs * ss
    t_all = tp + ts
    n_chunk = sp // CHUNK
    n_cache = cache_swa_k.shape[2]
    kvh = (SWA_KV_HEADS, SWA_HEAD_DIM)

    x = jnp.concatenate([x_prompt.reshape(tp, D_MODEL), x_sample.reshape(ts, D_MODEL)], axis=0)
    mem_x = mem_prompt.reshape(bp * N_MEM, D_MODEL)

    bias_p = _bias_table(rel_bias, CHUNK, SWA_BAND - CHUNK, SWA_BAND - CHUNK)
    bias_s = _bias_table(rel_bias, ss, n_cache + ss, n_cache)
    expand = _head_expansion()
    hist_p = jnp.zeros((bp, POOL_PAD, POOL_WIDTH), F32)
    hist_s = jnp.pad(state_pool, ((0, 0), (0, 0), (POOL_PAD - POOL_HIST, 0), (0, 0)))
    sinks = swa_sinks.astype(F32).reshape(depth, SWA_HEADS, 1, 1)
    sink_p = jnp.broadcast_to(sinks, (depth, SWA_HEADS, CHUNK, 128)).reshape(depth, SWA_HEADS * CHUNK, 128)
    sink_s = jnp.broadcast_to(sinks, (depth, SWA_HEADS, ss, 128)).reshape(depth, SWA_HEADS * ss, 128)
    norm3 = lambda g: g.reshape(depth, 1, D_MODEL)
    ffn_a_g, mix_g, mem_g, ffn_b_g = norm3(ffn_a_norm), norm3(mix_norm), norm3(mem_norm), norm3(ffn_b_norm)
    pool_s3 = pool_scale.reshape(depth, 1, POOL_WIDTH)
    cache_k4 = cache_swa_k.reshape(depth, bs, n_cache, SWA_KV_WIDTH)
    cache_v4 = cache_swa_v.reshape(depth, bs, n_cache, SWA_KV_WIDTH)
    cache_mk3 = cache_mem_k.reshape(depth, bs * N_MEM, MEM_WIDTH)
    cache_mv3 = cache_mem_v.reshape(depth, bs * N_MEM, MEM_WIDTH)

    p_k, p_v, p_pool, p_mk, p_mv, s_k, s_v, s_pool = ([] for _ in range(8))
    for l in range(depth):
        x = _ffn_half(x, ffn_a_g, ffn_a_w_in, ffn_a_w_out, l)

        h = _rmsnorm(x, (mix_g, l), BF16)
        u = _proj(h, w_in, l, OFF_U, POOL_WIDTH, F32)
        q = _proj(h, w_in, l, OFF_Q, SWA_WIDTH, BF16, scale=SWA_HEAD_DIM ** -0.5)
        kv, kt, vt = _proj_kv(h, w_in, l, expand)
        mq = _proj(h, w_in, l, OFF_MQ, MEM_WIDTH, BF16, scale=MEM_HEAD_DIM ** -0.5)
        gates = _proj(h, w_in, l, OFF_GATE, 3 * D_MODEL, F32)

        y_pool = _pool(u, hist_p, pool_w, pool_s3, l, 0, bp, sp, 1, 0, t_all)
        y_pool = _pool(u, hist_s, pool_w, pool_s3, l, POOL_HIST, bs, ss, bs, tp, t_all, prev=y_pool)
        u_s = u[tp:].reshape(bs, ss, POOL_WIDTH)
        p_pool.append(u[:tp].reshape(bp, sp, POOL_WIDTH)[:, sp - POOL_HIST:])
        s_pool.append(jnp.concatenate([state_pool[l], u_s], axis=1)[:, ss:])

        y_swa = _swa_prompt(q, kt, vt, bias_p, sink_p, l, bp, n_chunk)
        y_swa = _swa_sample(q, cache_k4, cache_v4, kv, expand, bias_s, sink_s, l, tp, bs, ss, y_swa)
        k_new, v_new = kv[:, :SWA_KV_WIDTH], kv[:, SWA_KV_WIDTH:]
        p_k.append(k_new[:tp].reshape(bp, sp, *kvh)[:, sp - SWA_WINDOW:])
        p_v.append(v_new[:tp].reshape(bp, sp, *kvh)[:, sp - SWA_WINDOW:])
        s_k.append(jnp.concatenate([cache_k4[l][:, ss:], k_new[tp:].reshape(bs, ss, SWA_KV_WIDTH)],
                                   axis=1).reshape(bs, n_cache, *kvh))
        s_v.append(jnp.concatenate([cache_v4[l][:, ss:], v_new[tp:].reshape(bs, ss, SWA_KV_WIDTH)],
                                   axis=1).reshape(bs, n_cache, *kvh))

        mkv = _proj(_rmsnorm(mem_x, (mem_g, l), BF16, bm=256), w_mem_kv, l, 0, 2 * MEM_WIDTH, F32, bm=256)
        y_mem = _mem_attend(mq, pl.BlockSpec((N_MEM, MEM_WIDTH), lambda b, s: (b, 0)),
                            pl.BlockSpec((N_MEM, MEM_WIDTH), lambda b, s: (b, 1)),
                            mkv, mkv, 0, bp, sp, 512)
        y_mem = _mem_attend(mq, pl.BlockSpec((None, N_MEM, MEM_WIDTH), lambda b, s: (l, b, 0)),
                            pl.BlockSpec((None, N_MEM, MEM_WIDTH), lambda b, s: (l, b, 0)),
                            cache_mk3, cache_mv3, tp, bs, ss, ss, prev=y_mem)
        p_mk.append(mkv[:, :MEM_WIDTH].reshape(bp, N_MEM, MEM_HEADS, MEM_HEAD_DIM))
        p_mv.append(mkv[:, MEM_WIDTH:].reshape(bp, N_MEM, MEM_HEADS, MEM_HEAD_DIM))

        m = _merge(gates, y_pool, y_swa, y_mem, w_branch_pool, w_branch_swa, w_branch_mem, l)
        x = _proj_resid(m, w_out, l, x)

        x = _ffn_half(x, ffn_b_g, ffn_b_w_in, ffn_b_w_out, l)

    fin = (final_norm.reshape(1, D_MODEL), None)
    y_p = _rmsnorm(x, fin, F32, bm=256, row0=0, rows=tp)
    y_s = _rmsnorm(x, fin, F32, bm=256, row0=tp, rows=ts)
    return (y_p.reshape(bp, sp, D_MODEL), y_s.reshape(bs, ss, D_MODEL),
            jnp.stack(p_k), jnp.stack(p_v), jnp.stack(p_pool), jnp.stack(p_mk), jnp.stack(p_mv),
            jnp.stack(s_k), jnp.stack(s_v), jnp.stack(s_pool))
```

```python
import functools
import math

import jax
import jax.numpy as jnp
from jax import lax
from jax.experimental import pallas as pl
from jax.experimental.pallas import tpu as pltpu

F32 = jnp.float32
BF16 = jnp.bfloat16

D_MODEL = 4096
D_FF = 11008
CHUNK = 64
EPS = 1e-6
POOL_WINDOWS = (2, 4, 8, 16)
POOL_GROUP = 256
POOL_WIDTH = 1024
POOL_HIST = 15
POOL_PAD = 16
SWA_WINDOW = 128
SWA_HEAD_DIM = 64
SWA_HEADS = 32
SWA_KV_HEADS = 4
SWA_GROUP = 8
SWA_WIDTH = 2048
SWA_KV_WIDTH = 256
SWA_SPAN = SWA_GROUP * SWA_HEAD_DIM
SWA_BAND = 256
N_MEM = 256
MEM_HEADS = 4
MEM_WIDTH = 1024
MEM_HEAD_DIM = 256
REL_BUCKETS = 32
REL_MAX_DIST = 128
NEG = -1e30

OFF_U = 0
OFF_Q = OFF_U + POOL_WIDTH
OFF_KV = OFF_Q + SWA_WIDTH
OFF_MQ = OFF_KV + 2 * SWA_KV_WIDTH
OFF_GATE = OFF_MQ + MEM_WIDTH

VMEM_LIMIT_BYTES = 58 * 1024 * 1024
BM = 1088
WB = 512
NW = 1
BM_MERGE = 544
FB = 256
BM_DOWN = 544
BN_DOWN = 512


def _cparams(n_axes):
    return pltpu.CompilerParams(
        dimension_semantics=("arbitrary",) * n_axes,
        vmem_limit_bytes=VMEM_LIMIT_BYTES,
    )


def _any_spec():
    return pl.BlockSpec(memory_space=pl.ANY)


def _rms_kernel(x_ref, g_ref, o_ref):
    x = x_ref[...]
    ms = jnp.mean(x * x, axis=-1, keepdims=True)
    o_ref[...] = (x * lax.rsqrt(ms + EPS) * g_ref[...]).astype(o_ref.dtype)


def _rmsnorm(x, g, out_dtype, bm=544, row0=0, rows=None):
    d = x.shape[1]
    rows = x.shape[0] if rows is None else rows
    blk0 = row0 // bm
    g_arr, l = g
    if l is None:
        gspec = pl.BlockSpec((1, d), lambda i: (0, 0))
    else:
        gspec = pl.BlockSpec((None, 1, d), lambda i: (l, 0, 0))
    return pl.pallas_call(
        _rms_kernel,
        out_shape=jax.ShapeDtypeStruct((rows, d), out_dtype),
        grid=(rows // bm,),
        in_specs=[pl.BlockSpec((bm, d), lambda i: (blk0 + i, 0)), gspec],
        out_specs=pl.BlockSpec((bm, d), lambda i: (i, 0)),
        compiler_params=_cparams(1),
        name="rmsnorm",
    )(x, g_arr)


def _rms_join_kernel(xp_ref, xs_ref, g_ref, h_ref, x_ref, *, n_first):
    def emit(src_ref):
        x = src_ref[...]
        ms = jnp.mean(x * x, axis=-1, keepdims=True)
        h_ref[...] = (x * lax.rsqrt(ms + EPS) * g_ref[...]).astype(h_ref.dtype)
        x_ref[...] = x

    i = pl.program_id(0)

    @pl.when(i < n_first)
    def _():
        emit(xp_ref)

    @pl.when(i >= n_first)
    def _():
        emit(xs_ref)


def _rmsnorm_join(xp, xs, g, l, bm=256):
    d = xp.shape[1]
    n_first = xp.shape[0] // bm
    n_second = xs.shape[0] // bm
    rows = xp.shape[0] + xs.shape[0]
    out = pl.BlockSpec((bm, d), lambda i: (i, 0))
    return pl.pallas_call(
        functools.partial(_rms_join_kernel, n_first=n_first),
        out_shape=(jax.ShapeDtypeStruct((rows, d), BF16), jax.ShapeDtypeStruct((rows, d), F32)),
        grid=(n_first + n_second,),
        in_specs=[pl.BlockSpec((bm, d), lambda i: (jnp.minimum(i, n_first - 1), 0)),
                  pl.BlockSpec((bm, d), lambda i: (jnp.maximum(i - n_first, 0), 0)),
                  pl.BlockSpec((None, 1, d), lambda i: (l, 0, 0))],
        out_specs=(out, out),
        compiler_params=_cparams(1),
        name="rmsnorm_join",
    )(xp, xs, g)


def _stage_weights(w_refs, wb_ref):
    @pl.when(pl.program_id(1) == 0)
    def _():
        wd = w_refs[0].shape[1]
        for t, w in enumerate(w_refs):
            wb_ref[:, t * wd:(t + 1) * wd] = w[...].astype(BF16)


def _wspecs(k, wb, l, first, nw):
    return [pl.BlockSpec((None, k, wb), functools.partial(lambda j, i, t: (l, 0, first + nw * j + t), t=t))
            for t in range(nw)]


def _proj_kernel(*refs, nw, scale):
    a_ref, w_refs, o_ref, wb_ref = refs[0], refs[1:1 + nw], refs[1 + nw], refs[2 + nw]
    _stage_weights(w_refs, wb_ref)
    acc = jnp.dot(a_ref[...], wb_ref[...], preferred_element_type=F32)
    if scale != 1.0:
        acc = acc * scale
    o_ref[...] = acc.astype(o_ref.dtype)


def _proj(a, w, l, col0, ncols, out_dtype, scale=1.0, bm=BM, wb=WB, nw=NW):
    m, k = a.shape
    bn = wb * nw
    return pl.pallas_call(
        functools.partial(_proj_kernel, nw=nw, scale=scale),
        out_shape=jax.ShapeDtypeStruct((m, ncols), out_dtype),
        grid=(ncols // bn, m // bm),
        in_specs=[pl.BlockSpec((bm, k), lambda j, i: (i, 0))] + _wspecs(k, wb, l, col0 // wb, nw),
        out_specs=pl.BlockSpec((bm, bn), lambda j, i: (i, j)),
        scratch_shapes=[pltpu.VMEM((k, bn), BF16)],
        compiler_params=_cparams(2),
        name="proj",
    )(a, *([w] * nw))


def _proj_kv_kernel(a_ref, w_ref, e_ref, kv_ref, kt_ref, vt_ref, wb_ref):
    _stage_weights((w_ref,), wb_ref)
    acc = jnp.dot(a_ref[...], wb_ref[...], preferred_element_type=F32)
    kv_ref[...] = acc
    kv = acc.astype(BF16)
    kt_ref[...] = jnp.dot(kv[:, :SWA_KV_WIDTH], e_ref[...], preferred_element_type=F32).astype(BF16)
    vt_ref[...] = jnp.dot(kv[:, SWA_KV_WIDTH:], e_ref[...], preferred_element_type=F32).astype(BF16)


def _proj_kv(a, w, l, expand, bm=BM // 2):
    m, k = a.shape
    n = 2 * SWA_KV_WIDTH
    row = lambda width: pl.BlockSpec((bm, width), lambda j, i: (i, 0))
    return pl.pallas_call(
        _proj_kv_kernel,
        out_shape=(jax.ShapeDtypeStruct((m, n), F32),
                   jax.ShapeDtypeStruct((m, SWA_WIDTH), BF16),
                   jax.ShapeDtypeStruct((m, SWA_WIDTH), BF16)),
        grid=(1, m // bm),
        in_specs=[row(k)] + _wspecs(k, n, l, OFF_KV // n, 1)
        + [pl.BlockSpec((SWA_KV_WIDTH, SWA_WIDTH), lambda j, i: (0, 0))],
        out_specs=(row(n), row(SWA_WIDTH), row(SWA_WIDTH)),
        scratch_shapes=[pltpu.VMEM((k, n), BF16)],
        compiler_params=_cparams(2),
        name="proj_kv",
    )(a, w, expand)


def _proj_resid_kernel(*refs, nw):
    a_ref, w_refs = refs[0], refs[1:1 + nw]
    r_ref, o_ref, wb_ref = refs[1 + nw:]
    _stage_weights(w_refs, wb_ref)
    o_ref[...] = r_ref[...] + jnp.dot(a_ref[...], wb_ref[...], preferred_element_type=F32)


def _proj_resid(a, w, l, resid, bm=BM, wb=WB, nw=NW):
    m, k = a.shape
    n = w.shape[2]
    bn = wb * nw
    tile = pl.BlockSpec((bm, bn), lambda j, i: (i, j))
    return pl.pallas_call(
        functools.partial(_proj_resid_kernel, nw=nw),
        out_shape=jax.ShapeDtypeStruct((m, n), F32),
        grid=(n // bn, m // bm),
        in_specs=[pl.BlockSpec((bm, k), lambda j, i: (i, 0))] + _wspecs(k, wb, l, 0, nw) + [tile],
        out_specs=tile,
        scratch_shapes=[pltpu.VMEM((k, bn), BF16)],
        compiler_params=_cparams(2),
        name="proj_resid",
    )(a, *([w] * nw), resid)


def _swiglu_kernel(a_ref, wg_ref, wu_ref, wo_ref, o_ref, wob_ref, wb_ref):
    _stage_weights((wg_ref, wu_ref), wb_ref)
    wob_ref[...] = wo_ref[...].astype(BF16)
    acc = jnp.dot(a_ref[...], wb_ref[...], preferred_element_type=F32)
    half = acc.shape[1] // 2
    gate, up = acc[:, :half], acc[:, half:]
    o_ref[...] = (gate * jax.nn.sigmoid(gate) * up).astype(o_ref.dtype)


def _ffn_up(a, w_in, w_out, l, bm=BM, fb=FB):
    m, k = a.shape
    nf = D_FF // fb
    ni = m // bm
    n_out = w_out.shape[2]
    oc = n_out // ni
    return pl.pallas_call(
        _swiglu_kernel,
        out_shape=(jax.ShapeDtypeStruct((m, D_FF), BF16),
                   jax.ShapeDtypeStruct((D_FF, n_out), BF16)),
        grid=(nf, ni),
        in_specs=[pl.BlockSpec((bm, k), lambda j, i: (i, 0)),
                  pl.BlockSpec((None, k, fb), lambda j, i: (l, 0, j)),
                  pl.BlockSpec((None, k, fb), lambda j, i: (l, 0, nf + j)),
                  pl.BlockSpec((None, fb, oc), lambda j, i: (l, j, i))],
        out_specs=(pl.BlockSpec((bm, fb), lambda j, i: (i, j)),
                   pl.BlockSpec((fb, oc), lambda j, i: (j, i))),
        scratch_shapes=[pltpu.VMEM((k, 2 * fb), BF16)],
        compiler_params=_cparams(2),
        name="ffn_up",
    )(a, w_in, w_in, w_out)


def _ffn_down_kernel(a_ref, w_ref, r_ref, o_ref):
    o_ref[...] = r_ref[...] + 0.5 * jnp.dot(a_ref[...], w_ref[...], preferred_element_type=F32)


def _ffn_down(a, w_bf16, resid, bm=BM_DOWN, bn=BN_DOWN):
    m, k = a.shape
    n = w_bf16.shape[1]
    tile = pl.BlockSpec((bm, bn), lambda i, j: (i, j))
    return pl.pallas_call(
        _ffn_down_kernel,
        out_shape=jax.ShapeDtypeStruct((m, n), F32),
        grid=(m // bm, n // bn),
        in_specs=[pl.BlockSpec((bm, k), lambda i, j: (i, 0)),
                  pl.BlockSpec((k, bn), lambda i, j: (0, j)), tile],
        out_specs=tile,
        compiler_params=_cparams(2),
        name="ffn_down",
    )(a, w_bf16, resid)


def _merge_kernel(yp_ref, ya_ref, ym_ref, wp_ref, wa_ref, wm_ref, gp_ref, ga_ref, gm_ref,
                  o_ref, wpb_ref, wab_ref, wmb_ref):
    @pl.when(pl.program_id(1) == 0)
    def _():
        wpb_ref[...] = wp_ref[...].astype(BF16)
        wab_ref[...] = wa_ref[...].astype(BF16)
        wmb_ref[...] = wm_ref[...].astype(BF16)

    m = jax.nn.sigmoid(gp_ref[...]) * jnp.dot(yp_ref[...], wpb_ref[...], preferred_element_type=F32)
    m = m + jax.nn.sigmoid(ga_ref[...]) * jnp.dot(ya_ref[...], wab_ref[...], preferred_element_type=F32)
    m = m + jax.nn.sigmoid(gm_ref[...]) * jnp.dot(ym_ref[...], wmb_ref[...], preferred_element_type=F32)
    o_ref[...] = m.astype(o_ref.dtype)


def _merge(gates, y_pool, y_swa, y_mem, w_bp, w_bs, w_bm, l, bm=BM_MERGE, bn=WB):
    m = gates.shape[0]
    nb = D_MODEL // bn
    act = lambda k: pl.BlockSpec((bm, k), lambda j, i: (i, 0))
    wgt = lambda k: pl.BlockSpec((None, k, bn), lambda j, i: (l, 0, j))
    gate = lambda t: pl.BlockSpec((bm, bn), lambda j, i: (i, j + t * nb))
    return pl.pallas_call(
        _merge_kernel,
        out_shape=jax.ShapeDtypeStruct((m, D_MODEL), BF16),
        grid=(nb, m // bm),
        in_specs=[act(POOL_WIDTH), act(SWA_WIDTH), act(MEM_WIDTH),
                  wgt(POOL_WIDTH), wgt(SWA_WIDTH), wgt(MEM_WIDTH),
                  gate(0), gate(1), gate(2)],
        out_specs=pl.BlockSpec((bm, bn), lambda j, i: (i, j)),
        scratch_shapes=[pltpu.VMEM((POOL_WIDTH, bn), BF16), pltpu.VMEM((SWA_WIDTH, bn), BF16),
                        pltpu.VMEM((MEM_WIDTH, bn), BF16)],
        compiler_params=_cparams(2),
        name="merge",
    )(y_pool, y_swa, y_mem, w_bp, w_bs, w_bm, gates, gates, gates)


def _pool_kernel(*refs, bb, s_len, n_hist, rc):
    u_ref, h_ref, w_ref, s_ref = refs[:4]
    o_ref, full_ref = refs[-2:]
    full_ref[:, 0:POOL_PAD, :] = h_ref[...]
    full_ref[:, POOL_PAD:, :] = u_ref[...].reshape(bb, s_len, POOL_WIDTH)
    for g, win in enumerate(POOL_WINDOWS):
        lanes = slice(g * POOL_GROUP, (g + 1) * POOL_GROUP)
        wg = w_ref[g].astype(BF16)
        sc = s_ref[:, lanes]
        for r0 in range(0, s_len, rc):
            base = POOL_PAD + r0
            cur = full_ref[:, base:base + rc, lanes]
            acc = cur
            for d in range(1, win):
                acc = acc + full_ref[:, base - d:base - d + rc, lanes]
            t = lax.broadcasted_iota(jnp.int32, (bb, rc, POOL_GROUP), 1) + (r0 + n_hist + 1)
            cnt = jnp.minimum(t, win).astype(F32)
            mixed = (acc / cnt - cur).reshape(bb * rc, POOL_GROUP).astype(BF16)
            y = jnp.dot(mixed, wg, preferred_element_type=F32) * sc
            o_ref[r0:r0 + bb * rc, lanes] = y.astype(o_ref.dtype)


def _pool(u, hist, pool_w, pool_scale, l, n_hist, n_batch, s_len, bb, row0, out_rows, prev=None):
    rows = bb * s_len
    blk0 = row0 // rows
    rc = min(s_len, 256)
    assert bb == 1 or rc == s_len
    if hist.ndim == 4:
        hspec = pl.BlockSpec((None, bb, POOL_PAD, POOL_WIDTH), lambda b: (l, b, 0, 0))
    else:
        hspec = pl.BlockSpec((bb, POOL_PAD, POOL_WIDTH), lambda b: (b, 0, 0))
    in_specs = [pl.BlockSpec((rows, POOL_WIDTH), lambda b: (blk0 + b, 0)), hspec,
                pl.BlockSpec((None, len(POOL_WINDOWS), POOL_GROUP, POOL_GROUP), lambda b: (l, 0, 0, 0)),
                pl.BlockSpec((None, 1, POOL_WIDTH), lambda b: (l, 0, 0))]
    args = [u, hist, pool_w, pool_scale]
    aliases = {}
    if prev is not None:
        in_specs.append(_any_spec())
        args.append(prev)
        aliases = {4: 0}
    return pl.pallas_call(
        functools.partial(_pool_kernel, bb=bb, s_len=s_len, n_hist=n_hist, rc=rc),
        out_shape=jax.ShapeDtypeStruct((out_rows, POOL_WIDTH), BF16),
        grid=(n_batch // bb,),
        in_specs=in_specs,
        out_specs=pl.BlockSpec((rows, POOL_WIDTH), lambda b: (blk0 + b, 0)),
        scratch_shapes=[pltpu.VMEM((bb, POOL_PAD + s_len, POOL_WIDTH), F32)],
        input_output_aliases=aliases,
        compiler_params=_cparams(1),
        name="pool",
    )(*args)


def _rel_bucket(rel):
    half = REL_BUCKETS // 2
    max_exact = half // 2
    n = jnp.abs(rel)
    nf = jnp.maximum(n, 1).astype(jnp.float32)
    large = max_exact + (jnp.log(nf / max_exact) / math.log(REL_MAX_DIST / max_exact)
                         * (half - max_exact)).astype(jnp.int32)
    large = jnp.minimum(large, half - 1)
    return jnp.where(rel > 0, half, 0) + jnp.where(n < max_exact, n, large)


def _bias_kernel(tab_ref, idx_ref, o_ref):
    h = pl.program_id(0)
    idx = idx_ref[...]
    acc = jnp.where(idx < 0, NEG, 0.0).astype(F32)
    for b in range(REL_BUCKETS):
        acc = acc + jnp.where(idx == b, tab_ref[b, h], 0.0)
    o_ref[0] = acc


def _bias_table(table, sq, n_keys, key_zero):
    band_lo = key_zero - SWA_WINDOW
    j = jnp.arange(SWA_BAND)[None, :]
    qi = jnp.arange(sq)[:, None]
    idx = _rel_bucket(j - key_zero - qi)
    idx = jnp.where((j >= band_lo) & (j < band_lo + n_keys), idx, -1).astype(jnp.int32)
    return pl.pallas_call(
        _bias_kernel,
        out_shape=jax.ShapeDtypeStruct((SWA_HEADS, sq, SWA_BAND), F32),
        grid=(SWA_HEADS,),
        in_specs=[pl.BlockSpec(memory_space=pltpu.SMEM),
                  pl.BlockSpec((sq, SWA_BAND), lambda h: (0, 0))],
        out_specs=pl.BlockSpec((1, sq, SWA_BAND), lambda h: (h, 0, 0)),
        compiler_params=_cparams(1),
        name="rel_bias",
    )(table, idx)


def _swa_core(q, kband, vband, bias, sink_col, col_ok, sq):
    rows = SWA_GROUP * sq
    lane_grp = lax.shift_right_logical(lax.broadcasted_iota(jnp.int32, (sq, SWA_SPAN), 1), 6)
    parts = []
    for h in range(SWA_KV_HEADS):
        lanes = slice(h * SWA_SPAN, (h + 1) * SWA_SPAN)
        qs = q[:, lanes]
        qstack = jnp.concatenate(
            [jnp.where(lane_grp == g, qs, jnp.zeros_like(qs)) for g in range(SWA_GROUP)], axis=0)
        parts.append(lax.dot_general(qstack, kband[:, lanes], (((1,), (1,)), ((), ())),
                                     preferred_element_type=F32))
    s = jnp.concatenate(parts, axis=0) + bias
    if col_ok is not None:
        s = jnp.where(col_ok, s, NEG)
    m = jnp.maximum(jnp.max(s, axis=1, keepdims=True), sink_col)
    e = jnp.exp(s - m)
    z = jnp.sum(e, axis=1, keepdims=True) + jnp.exp(sink_col - m)
    p = (e / z).astype(BF16)
    outs = []
    for h in range(SWA_KV_HEADS):
        lanes = slice(h * SWA_SPAN, (h + 1) * SWA_SPAN)
        r = jnp.dot(p[h * rows:(h + 1) * rows], vband[:, lanes], preferred_element_type=F32)
        o = jnp.where(lane_grp == 0, r[0:sq], 0.0)
        for g in range(1, SWA_GROUP):
            o = o + jnp.where(lane_grp == g, r[g * sq:(g + 1) * sq], 0.0)
        outs.append(o)
    return jnp.concatenate(outs, axis=1)


def _swa_prompt_kernel(*refs, n_kb):
    q_ref = refs[0]
    k_refs = refs[1:1 + n_kb]
    v_refs = refs[1 + n_kb:1 + 2 * n_kb]
    b_ref, s_ref, o_ref = refs[1 + 2 * n_kb:]
    sq = q_ref.shape[0]
    kband = jnp.concatenate([r[...] for r in k_refs], axis=0)
    vband = jnp.concatenate([r[...] for r in v_refs], axis=0)
    lo = jnp.maximum(0, (n_kb - 1 - pl.program_id(1)) * CHUNK)
    col_ok = lax.broadcasted_iota(jnp.int32, (SWA_HEADS * sq, SWA_BAND), 1) >= lo
    bias = b_ref[...].reshape(SWA_HEADS * sq, SWA_BAND)
    o = _swa_core(q_ref[...], kband, vband, bias, s_ref[:, 0:1], col_ok, sq)
    o_ref[...] = o.astype(o_ref.dtype)


def _swa_prompt(q, kt, vt, bias, sink_cols, l, n_batch, n_chunk):
    sq = CHUNK
    n_kb = SWA_BAND // CHUNK
    qspec = pl.BlockSpec((sq, SWA_WIDTH), lambda b, c: (b * n_chunk + c, 0))

    def band(d):
        return pl.BlockSpec((CHUNK, SWA_WIDTH),
                            lambda b, c: (b * n_chunk + jnp.maximum(c - (n_kb - 1) + d, 0), 0))

    return pl.pallas_call(
        functools.partial(_swa_prompt_kernel, n_kb=n_kb),
        out_shape=jax.ShapeDtypeStruct(q.shape, BF16),
        grid=(n_batch, n_chunk),
        in_specs=[qspec] + [band(d) for d in range(n_kb)] * 2
        + [pl.BlockSpec((SWA_HEADS, sq, SWA_BAND), lambda b, c: (0, 0, 0)),
           pl.BlockSpec((None, SWA_HEADS * sq, 128), lambda b, c: (l, 0, 0))],
        out_specs=qspec,
        compiler_params=_cparams(2),
        name="swa_prompt",
    )(q, *([kt] * n_kb), *([vt] * n_kb), bias, sink_cols)


def _swa_sample_kernel(q_ref, ck_ref, cv_ref, kvn_ref, e_ref, b_ref, s_ref, prev_ref, o_ref):
    del prev_ref
    sq = q_ref.shape[0]
    n_cache = ck_ref.shape[0]
    kvn = kvn_ref[...]
    zpad = jnp.zeros((SWA_BAND - n_cache - sq, SWA_KV_WIDTH), F32)
    kk = jnp.concatenate([ck_ref[...], kvn[:, :SWA_KV_WIDTH], zpad], axis=0).astype(BF16)
    vv = jnp.concatenate([cv_ref[...], kvn[:, SWA_KV_WIDTH:], zpad], axis=0).astype(BF16)
    kband = jnp.dot(kk, e_ref[...], preferred_element_type=F32).astype(BF16)
    vband = jnp.dot(vv, e_ref[...], preferred_element_type=F32).astype(BF16)
    bias = b_ref[...].reshape(SWA_HEADS * sq, SWA_BAND)
    o = _swa_core(q_ref[...], kband, vband, bias, s_ref[:, 0:1], None, sq)
    o_ref[...] = o.astype(o_ref.dtype)


def _swa_sample(q, cache_k, cache_v, kv, expand, bias, sink_cols, l, row0, n_batch, sq, prev):
    blk0 = row0 // sq
    n_cache = cache_k.shape[2]
    cspec = pl.BlockSpec((None, None, n_cache, SWA_KV_WIDTH), lambda b: (l, b, 0, 0))
    return pl.pallas_call(
        _swa_sample_kernel,
        out_shape=jax.ShapeDtypeStruct(prev.shape, BF16),
        grid=(n_batch,),
        in_specs=[pl.BlockSpec((sq, SWA_WIDTH), lambda b: (blk0 + b, 0)), cspec, cspec,
                  pl.BlockSpec((sq, 2 * SWA_KV_WIDTH), lambda b: (blk0 + b, 0)),
                  pl.BlockSpec((SWA_KV_WIDTH, SWA_WIDTH), lambda b: (0, 0)),
                  pl.BlockSpec((SWA_HEADS, sq, SWA_BAND), lambda b: (0, 0, 0)),
                  pl.BlockSpec((None, SWA_HEADS * sq, 128), lambda b: (l, 0, 0)),
                  _any_spec()],
        out_specs=pl.BlockSpec((sq, SWA_WIDTH), lambda b: (blk0 + b, 0)),
        input_output_aliases={7: 0},
        compiler_params=_cparams(1),
        name="swa_sample",
    )(q, cache_k, cache_v, kv, expand, bias, sink_cols, prev)


def _mem_kernel(*refs):
    q_ref, k_ref, v_ref = refs[:3]
    o_ref = refs[-1]
    for h in range(MEM_HEADS):
        lanes = slice(h * MEM_HEAD_DIM, (h + 1) * MEM_HEAD_DIM)
        if len(k_ref.shape) == 3:
            kh = k_ref[:, h, :].astype(BF16)
            vh = v_ref[:, h, :].astype(BF16)
        else:
            kh = k_ref[:, lanes].astype(BF16)
            vh = v_ref[:, lanes].astype(BF16)
        s = lax.dot_general(q_ref[:, lanes], kh, (((1,), (1,)), ((), ())),
                            preferred_element_type=F32)
        m = jnp.max(s, axis=1, keepdims=True)
        e = jnp.exp(s - m)
        p = (e / jnp.sum(e, axis=1, keepdims=True)).astype(BF16)
        o_ref[:, lanes] = jnp.dot(p, vh, preferred_element_type=F32).astype(o_ref.dtype)


def _mem_attend(mq, kspec, vspec, k, v, row0, n_batch, s_len, bq, prev=None):
    per = s_len // bq
    blk0 = row0 // bq
    qspec = pl.BlockSpec((bq, MEM_WIDTH), lambda b, s: (blk0 + b * per + s, 0))
    in_specs = [qspec, kspec, vspec]
    args = [mq, k, v]
    aliases = {}
    if prev is not None:
        in_specs.append(_any_spec())
        args.append(prev)
        aliases = {3: 0}
    return pl.pallas_call(
        _mem_kernel,
        out_shape=jax.ShapeDtypeStruct(mq.shape, BF16),
        grid=(n_batch, per),
        in_specs=in_specs,
        out_specs=qspec,
        input_output_aliases=aliases,
        compiler_params=_cparams(2),
        name="mem_attend",
    )(*args)


def _head_expansion():
    src = jnp.arange(SWA_KV_WIDTH)
    dst = jnp.arange(SWA_WIDTH)
    dst_src = (dst // SWA_SPAN) * SWA_HEAD_DIM + dst % SWA_HEAD_DIM
    return (src[:, None] == dst_src[None, :]).astype(BF16)


def _ffn_half(h, x, w_in, w_out, l):
    hh, w_out_bf16 = _ffn_up(h, w_in, w_out, l)
    return _ffn_down(hh, w_out_bf16, x)


def kernel(x_prompt, x_sample, cache_swa_k, cache_swa_v, state_pool, cache_mem_k, cache_mem_v,
           mem_prompt, ffn_a_norm, ffn_a_w_in, ffn_a_w_out, mix_norm, mem_norm, w_in, pool_w,
           pool_scale, swa_sinks, rel_bias, w_mem_kv, w_branch_pool, w_branch_swa, w_branch_mem,
           w_out, ffn_b_norm, ffn_b_w_in, ffn_b_w_out, final_norm):
    bp, sp, _ = x_prompt.shape
    bs, ss, _ = x_sample.shape
    depth = w_in.shape[0]
    tp = bp * sp
    ts = bs * ss
    t_all = tp + ts
    n_chunk = sp // CHUNK
    n_cache = cache_swa_k.shape[2]
    kvh = (SWA_KV_HEADS, SWA_HEAD_DIM)

    mem_x = mem_prompt.reshape(bp * N_MEM, D_MODEL)

    bias_p = _bias_table(rel_bias, CHUNK, SWA_BAND - CHUNK, SWA_BAND - CHUNK)
    bias_s = _bias_table(rel_bias, ss, n_cache + ss, n_cache)
    expand = _head_expansion()
    hist_p = jnp.zeros((bp, POOL_PAD, POOL_WIDTH), F32)
    hist_s = jnp.pad(state_pool, ((0, 0), (0, 0), (POOL_PAD - POOL_HIST, 0), (0, 0)))
    sinks = swa_sinks.astype(F32).reshape(depth, SWA_HEADS, 1, 1)
    sink_p = jnp.broadcast_to(sinks, (depth, SWA_HEADS, CHUNK, 128)).reshape(depth, SWA_HEADS * CHUNK, 128)
    sink_s = jnp.broadcast_to(sinks, (depth, SWA_HEADS, ss, 128)).reshape(depth, SWA_HEADS * ss, 128)
    norm3 = lambda g: g.reshape(depth, 1, D_MODEL)
    ffn_a_g, mix_g, mem_g, ffn_b_g = norm3(ffn_a_norm), norm3(mix_norm), norm3(mem_norm), norm3(ffn_b_norm)
    pool_s3 = pool_scale.reshape(depth, 1, POOL_WIDTH)
    cache_k4 = cache_swa_k.reshape(depth, bs, n_cache, SWA_KV_WIDTH)
    cache_v4 = cache_swa_v.reshape(depth, bs, n_cache, SWA_KV_WIDTH)

    p_k, p_v, p_pool, p_mk, p_mv, s_k, s_v, s_pool = ([] for _ in range(8))
    h, x = _rmsnorm_join(x_prompt.reshape(tp, D_MODEL), x_sample.reshape(ts, D_MODEL), ffn_a_g, 0)
    for l in range(depth):
        if l > 0:
            h = _rmsnorm(x, (ffn_a_g, l), BF16)
        x = _ffn_half(h, x, ffn_a_w_in, ffn_a_w_out, l)

        h = _rmsnorm(x, (mix_g, l), BF16)
        u = _proj(h, w_in, l, OFF_U, POOL_WIDTH, F32)
        q = _proj(h, w_in, l, OFF_Q, SWA_WIDTH, BF16, scale=SWA_HEAD_DIM ** -0.5)
        kv, kt, vt = _proj_kv(h, w_in, l, expand)
        mq = _proj(h, w_in, l, OFF_MQ, MEM_WIDTH, BF16, scale=MEM_HEAD_DIM ** -0.5)
        gates = _proj(h, w_in, l, OFF_GATE, 3 * D_MODEL, F32, bm=BM // 2, nw=2)

        y_pool = _pool(u, hist_p, pool_w, pool_s3, l, 0, bp, sp, 1, 0, t_all)
        y_pool = _pool(u, hist_s, pool_w, pool_s3, l, POOL_HIST, bs, ss, bs, tp, t_all, prev=y_pool)
        u_s = u[tp:].reshape(bs, ss, POOL_WIDTH)
        p_pool.append(u[:tp].reshape(bp, sp, POOL_WIDTH)[:, sp - POOL_HIST:])
        s_pool.append(jnp.concatenate([state_pool[l], u_s], axis=1)[:, ss:])

        y_swa = _swa_prompt(q, kt, vt, bias_p, sink_p, l, bp, n_chunk)
        y_swa = _swa_sample(q, cache_k4, cache_v4, kv, expand, bias_s, sink_s, l, tp, bs, ss, y_swa)
        k_new, v_new = kv[:, :SWA_KV_WIDTH], kv[:, SWA_KV_WIDTH:]
        p_k.append(k_new[:tp].reshape(bp, sp, *kvh)[:, sp - SWA_WINDOW:])
        p_v.append(v_new[:tp].reshape(bp, sp, *kvh)[:, sp - SWA_WINDOW:])
        s_k.append(jnp.concatenate([cache_k4[l][:, ss:], k_new[tp:].reshape(bs, ss, SWA_KV_WIDTH)],
                                   axis=1).reshape(bs, n_cache, *kvh))
        s_v.append(jnp.concatenate([cache_v4[l][:, ss:], v_new[tp:].reshape(bs, ss, SWA_KV_WIDTH)],
                                   axis=1).reshape(bs, n_cache, *kvh))

        mkv = _proj(_rmsnorm(mem_x, (mem_g, l), BF16, bm=256), w_mem_kv, l, 0, 2 * MEM_WIDTH, F32, bm=256)
        y_mem = _mem_attend(mq, pl.BlockSpec((N_MEM, MEM_WIDTH), lambda b, s: (b, 0)),
                            pl.BlockSpec((N_MEM, MEM_WIDTH), lambda b, s: (b, 1)),
                            mkv, mkv, 0, bp, sp, 512)
        cache_spec = pl.BlockSpec((None, None, N_MEM, MEM_HEADS, MEM_HEAD_DIM),
                                  lambda b, s: (l, b, 0, 0, 0))
        y_mem = _mem_attend(mq, cache_spec, cache_spec, cache_mem_k, cache_mem_v, tp, bs, ss, ss,
                            prev=y_mem)
        p_mk.append(mkv[:, :MEM_WIDTH].reshape(bp, N_MEM, MEM_HEADS, MEM_HEAD_DIM))
        p_mv.append(mkv[:, MEM_WIDTH:].reshape(bp, N_MEM, MEM_HEADS, MEM_HEAD_DIM))

        m = _merge(gates, y_pool, y_swa, y_mem, w_branch_pool, w_branch_swa, w_branch_mem, l)
        x = _proj_resid(m, w_out, l, x)

        x = _ffn_half(_rmsnorm(x, (ffn_b_g, l), BF16), x, ffn_b_w_in, ffn_b_w_out, l)

    fin = (final_norm.reshape(1, D_MODEL), None)
    y_p = _rmsnorm(x, fin, F32, bm=256, row0=0, rows=tp)
    y_s = _rmsnorm(x, fin, F32, bm=256, row0=tp, rows=ts)
    return (y_p.reshape(bp, sp, D_MODEL), y_s.reshape(bs, ss, D_MODEL),
            jnp.stack(p_k), jnp.stack(p_v), jnp.stack(p_pool), jnp.stack(p_mk), jnp.stack(p_mv),
            jnp.stack(s_k), jnp.stack(s_v), jnp.stack(s_pool))
```

```python
import functools
import math

import jax
import jax.numpy as jnp
from jax import lax
from jax.experimental import pallas as pl
from jax.experimental.pallas import tpu as pltpu

F32 = jnp.float32
BF16 = jnp.bfloat16

D_MODEL = 4096
D_FF = 11008
CHUNK = 64
EPS = 1e-6
POOL_WINDOWS = (2, 4, 8, 16)
POOL_GROUP = 256
POOL_WIDTH = 1024
POOL_HIST = 15
POOL_PAD = 16
SWA_WINDOW = 128
SWA_HEAD_DIM = 64
SWA_HEADS = 32
SWA_KV_HEADS = 4
SWA_GROUP = 8
SWA_WIDTH = 2048
SWA_KV_WIDTH = 256
SWA_SPAN = SWA_GROUP * SWA_HEAD_DIM
SWA_BAND = 256
N_MEM = 256
MEM_HEADS = 4
MEM_WIDTH = 1024
MEM_HEAD_DIM = 256
REL_BUCKETS = 32
REL_MAX_DIST = 128
NEG = -1e30

OFF_U = 0
OFF_Q = OFF_U + POOL_WIDTH
OFF_KV = OFF_Q + SWA_WIDTH
OFF_MQ = OFF_KV + 2 * SWA_KV_WIDTH
OFF_GATE = OFF_MQ + MEM_WIDTH

VMEM_LIMIT_BYTES = 58 * 1024 * 1024
VMEM_LIMIT_STAGED_BYTES = 62 * 1024 * 1024
BM = 1088
WB = 512
NW = 1
BM_MERGE = 544
FB = 256
BM_DOWN = 544
BN_DOWN = 512
CAST_ROWS = 512


def _cparams(n_axes, vmem_limit_bytes=VMEM_LIMIT_BYTES):
    return pltpu.CompilerParams(
        dimension_semantics=("arbitrary",) * n_axes,
        vmem_limit_bytes=vmem_limit_bytes,
    )


def _any_spec():
    return pl.BlockSpec(memory_space=pl.ANY)


def _rms_kernel(x_ref, g_ref, o_ref):
    x = x_ref[...]
    ms = jnp.mean(x * x, axis=-1, keepdims=True)
    o_ref[...] = (x * lax.rsqrt(ms + EPS) * g_ref[...]).astype(o_ref.dtype)


def _rmsnorm(x, g, out_dtype, bm=544, row0=0, rows=None):
    d = x.shape[1]
    rows = x.shape[0] if rows is None else rows
    blk0 = row0 // bm
    g_arr, l = g
    if l is None:
        gspec = pl.BlockSpec((1, d), lambda i: (0, 0))
    else:
        gspec = pl.BlockSpec((None, 1, d), lambda i: (l, 0, 0))
    return pl.pallas_call(
        _rms_kernel,
        out_shape=jax.ShapeDtypeStruct((rows, d), out_dtype),
        grid=(rows // bm,),
        in_specs=[pl.BlockSpec((bm, d), lambda i: (blk0 + i, 0)), gspec],
        out_specs=pl.BlockSpec((bm, d), lambda i: (i, 0)),
        compiler_params=_cparams(1),
        name="rmsnorm",
    )(x, g_arr)


def _rms_join_kernel(xp_ref, xs_ref, g_ref, h_ref, x_ref, *, n_first):
    def emit(src_ref):
        x = src_ref[...]
        ms = jnp.mean(x * x, axis=-1, keepdims=True)
        h_ref[...] = (x * lax.rsqrt(ms + EPS) * g_ref[...]).astype(h_ref.dtype)
        x_ref[...] = x

    i = pl.program_id(0)

    @pl.when(i < n_first)
    def _():
        emit(xp_ref)

    @pl.when(i >= n_first)
    def _():
        emit(xs_ref)


def _rmsnorm_join(xp, xs, g, l, bm=256):
    d = xp.shape[1]
    n_first = xp.shape[0] // bm
    n_second = xs.shape[0] // bm
    rows = xp.shape[0] + xs.shape[0]
    out = pl.BlockSpec((bm, d), lambda i: (i, 0))
    return pl.pallas_call(
        functools.partial(_rms_join_kernel, n_first=n_first),
        out_shape=(jax.ShapeDtypeStruct((rows, d), BF16), jax.ShapeDtypeStruct((rows, d), F32)),
        grid=(n_first + n_second,),
        in_specs=[pl.BlockSpec((bm, d), lambda i: (jnp.minimum(i, n_first - 1), 0)),
                  pl.BlockSpec((bm, d), lambda i: (jnp.maximum(i - n_first, 0), 0)),
                  pl.BlockSpec((None, 1, d), lambda i: (l, 0, 0))],
        out_specs=(out, out),
        compiler_params=_cparams(1),
        name="rmsnorm_join",
    )(xp, xs, g)


def _stage_weights(w_refs, wb_ref):
    @pl.when(pl.program_id(1) == 0)
    def _():
        wd = w_refs[0].shape[1]
        for t, w in enumerate(w_refs):
            wb_ref[:, t * wd:(t + 1) * wd] = w[...].astype(BF16)


def _wspecs(k, wb, l, first, nw):
    return [pl.BlockSpec((None, k, wb), functools.partial(lambda j, i, t: (l, 0, first + nw * j + t), t=t))
            for t in range(nw)]


def _proj_kernel(*refs, nw, scale):
    a_ref, w_refs, o_ref, wb_ref = refs[0], refs[1:1 + nw], refs[1 + nw], refs[2 + nw]
    _stage_weights(w_refs, wb_ref)
    acc = jnp.dot(a_ref[...], wb_ref[...], preferred_element_type=F32)
    if scale != 1.0:
        acc = acc * scale
    o_ref[...] = acc.astype(o_ref.dtype)


def _proj(a, w, l, col0, ncols, out_dtype, scale=1.0, bm=BM, wb=WB, nw=NW):
    m, k = a.shape
    bn = wb * nw
    return pl.pallas_call(
        functools.partial(_proj_kernel, nw=nw, scale=scale),
        out_shape=jax.ShapeDtypeStruct((m, ncols), out_dtype),
        grid=(ncols // bn, m // bm),
        in_specs=[pl.BlockSpec((bm, k), lambda j, i: (i, 0))] + _wspecs(k, wb, l, col0 // wb, nw),
        out_specs=pl.BlockSpec((bm, bn), lambda j, i: (i, j)),
        scratch_shapes=[pltpu.VMEM((k, bn), BF16)],
        compiler_params=_cparams(2),
        name="proj",
    )(a, *([w] * nw))


def _proj_kv_kernel(a_ref, w_ref, e_ref, kv_ref, kt_ref, vt_ref, wb_ref):
    _stage_weights((w_ref,), wb_ref)
    acc = jnp.dot(a_ref[...], wb_ref[...], preferred_element_type=F32)
    kv_ref[...] = acc
    kv = acc.astype(BF16)
    kt_ref[...] = jnp.dot(kv[:, :SWA_KV_WIDTH], e_ref[...], preferred_element_type=F32).astype(BF16)
    vt_ref[...] = jnp.dot(kv[:, SWA_KV_WIDTH:], e_ref[...], preferred_element_type=F32).astype(BF16)


def _proj_kv(a, w, l, expand, bm=BM // 2):
    m, k = a.shape
    n = 2 * SWA_KV_WIDTH
    row = lambda width: pl.BlockSpec((bm, width), lambda j, i: (i, 0))
    return pl.pallas_call(
        _proj_kv_kernel,
        out_shape=(jax.ShapeDtypeStruct((m, n), F32),
                   jax.ShapeDtypeStruct((m, SWA_WIDTH), BF16),
                   jax.ShapeDtypeStruct((m, SWA_WIDTH), BF16)),
        grid=(1, m // bm),
        in_specs=[row(k)] + _wspecs(k, n, l, OFF_KV // n, 1)
        + [pl.BlockSpec((SWA_KV_WIDTH, SWA_WIDTH), lambda j, i: (0, 0))],
        out_specs=(row(n), row(SWA_WIDTH), row(SWA_WIDTH)),
        scratch_shapes=[pltpu.VMEM((k, n), BF16)],
        compiler_params=_cparams(2),
        name="proj_kv",
    )(a, w, expand)


def _proj_resid_kernel(*refs, nw):
    a_ref, w_refs = refs[0], refs[1:1 + nw]
    r_ref, o_ref, wb_ref = refs[1 + nw:]
    _stage_weights(w_refs, wb_ref)
    o_ref[...] = r_ref[...] + jnp.dot(a_ref[...], wb_ref[...], preferred_element_type=F32)


def _proj_resid(a, w, l, resid, bm=BM, wb=WB, nw=NW):
    m, k = a.shape
    n = w.shape[2]
    bn = wb * nw
    tile = pl.BlockSpec((bm, bn), lambda j, i: (i, j))
    return pl.pallas_call(
        functools.partial(_proj_resid_kernel, nw=nw),
        out_shape=jax.ShapeDtypeStruct((m, n), F32),
        grid=(n // bn, m // bm),
        in_specs=[pl.BlockSpec((bm, k), lambda j, i: (i, 0))] + _wspecs(k, wb, l, 0, nw) + [tile],
        out_specs=tile,
        scratch_shapes=[pltpu.VMEM((k, bn), BF16)],
        compiler_params=_cparams(2),
        name="proj_resid",
    )(a, *([w] * nw), resid)


def _staged_weights(i, j, nj, start_copies, wait_copies, stage_ref, wb_ref):
    @pl.when(i == 0)
    def _():
        @pl.when(j == 0)
        def _():
            start_copies(0)

        wait_copies(j)
        for r0 in range(0, wb_ref.shape[0], CAST_ROWS):
            wb_ref[r0:r0 + CAST_ROWS, :] = stage_ref[r0:r0 + CAST_ROWS, :].astype(BF16)

        @pl.when(j + 1 < nj)
        def _():
            start_copies(j + 1)


def _proj_staged_kernel(a_ref, w_hbm, o_ref, stage_ref, wb_ref, sem, *, l, nj, col0, scale):
    j = pl.program_id(0)
    i = pl.program_id(1)
    bn = o_ref.shape[1]

    def copy(jj):
        col = pl.multiple_of(col0 + jj * bn, WB)
        return pltpu.make_async_copy(w_hbm.at[l, :, pl.ds(col, bn)], stage_ref, sem.at[0])

    _staged_weights(i, j, nj, lambda jj: copy(jj).start(), lambda jj: copy(jj).wait(),
                    stage_ref, wb_ref)
    rc = a_ref.shape[0] // 4
    for r0 in range(0, a_ref.shape[0], rc):
        acc = jnp.dot(a_ref[r0:r0 + rc, :], wb_ref[...], preferred_element_type=F32)
        if scale != 1.0:
            acc = acc * scale
        o_ref[r0:r0 + rc, :] = acc.astype(o_ref.dtype)


def _proj_staged(a, w, l, col0, ncols, out_dtype, scale=1.0, bm=BM, bn=2 * WB):
    m, k = a.shape
    nj = ncols // bn
    assert col0 % WB == 0 and ncols % bn == 0
    return pl.pallas_call(
        functools.partial(_proj_staged_kernel, l=l, nj=nj, col0=col0, scale=scale),
        out_shape=jax.ShapeDtypeStruct((m, ncols), out_dtype),
        grid=(nj, m // bm),
        in_specs=[pl.BlockSpec((bm, k), lambda j, i: (i, 0)), _any_spec()],
        out_specs=pl.BlockSpec((bm, bn), lambda j, i: (i, j)),
        scratch_shapes=[pltpu.VMEM((k, bn), F32), pltpu.VMEM((k, bn), BF16),
                        pltpu.SemaphoreType.DMA((1,))],
        compiler_params=_cparams(2, VMEM_LIMIT_STAGED_BYTES),
        name="proj_staged",
    )(a, w)


def _swiglu_kernel(a_ref, w_hbm, wo_ref, o_ref, wob_ref, stage_ref, wb_ref, sem, *, l, nj, fn, tail):
    j = pl.program_id(0)
    i = pl.program_id(1)

    def copies(jj, width):
        col = pl.multiple_of(jj * fn, fn)
        return (pltpu.make_async_copy(w_hbm.at[l, :, pl.ds(col, width)],
                                      stage_ref.at[:, :width], sem.at[0]),
                pltpu.make_async_copy(w_hbm.at[l, :, pl.ds(D_FF + col, width)],
                                      stage_ref.at[:, fn:fn + width], sem.at[1]))

    def each(jj, method):
        @pl.when(jj < nj - 1)
        def _():
            for c in copies(jj, fn):
                getattr(c, method)()

        @pl.when(jj == nj - 1)
        def _():
            for c in copies(jj, tail):
                getattr(c, method)()

    _staged_weights(i, j, nj, lambda jj: each(jj, "start"), lambda jj: each(jj, "wait"),
                    stage_ref, wb_ref)
    wob_ref[...] = wo_ref[...].astype(BF16)

    def swiglu(width):
        rc = a_ref.shape[0] // 4
        for r0 in range(0, a_ref.shape[0], rc):
            a = a_ref[r0:r0 + rc, :]
            if width == fn:
                acc = jnp.dot(a, wb_ref[...], preferred_element_type=F32)
                gate, up = acc[:, :fn], acc[:, fn:]
            else:
                gate = jnp.dot(a, wb_ref[:, :width], preferred_element_type=F32)
                up = jnp.dot(a, wb_ref[:, fn:fn + width], preferred_element_type=F32)
                o_ref[r0:r0 + rc, width:] = jnp.zeros((rc, fn - width), o_ref.dtype)
            o_ref[r0:r0 + rc, :width] = (gate * jax.nn.sigmoid(gate) * up).astype(o_ref.dtype)

    @pl.when(j < nj - 1)
    def _():
        swiglu(fn)

    @pl.when(j == nj - 1)
    def _():
        swiglu(tail)


def _ffn_up(a, w_in, w_out, l, bm=BM, fn=2 * FB):
    m, k = a.shape
    nj = pl.cdiv(D_FF, fn)
    tail = D_FF - (nj - 1) * fn
    ni = m // bm
    n_out = w_out.shape[2]
    oc = n_out // ni
    return pl.pallas_call(
        functools.partial(_swiglu_kernel, l=l, nj=nj, fn=fn, tail=tail),
        out_shape=(jax.ShapeDtypeStruct((m, nj * fn), BF16),
                   jax.ShapeDtypeStruct((D_FF, n_out), BF16)),
        grid=(nj, ni),
        in_specs=[pl.BlockSpec((bm, k), lambda j, i: (i, 0)),
                  _any_spec(),
                  pl.BlockSpec((None, fn, oc), lambda j, i: (l, j, i))],
        out_specs=(pl.BlockSpec((bm, fn), lambda j, i: (i, j)),
                   pl.BlockSpec((fn, oc), lambda j, i: (j, i))),
        scratch_shapes=[pltpu.VMEM((k, 2 * fn), F32), pltpu.VMEM((k, 2 * fn), BF16),
                        pltpu.SemaphoreType.DMA((2,))],
        compiler_params=_cparams(2, VMEM_LIMIT_STAGED_BYTES),
        name="ffn_up",
    )(a, w_in, w_out)


def _ffn_down_kernel(a_ref, w_ref, r_ref, o_ref):
    o_ref[...] = r_ref[...] + 0.5 * jnp.dot(a_ref[...], w_ref[...], preferred_element_type=F32)


def _ffn_down(a, w_bf16, resid, bm=BM_DOWN, bn=BN_DOWN):
    m = a.shape[0]
    k, n = w_bf16.shape
    tile = pl.BlockSpec((bm, bn), lambda i, j: (i, j))
    return pl.pallas_call(
        _ffn_down_kernel,
        out_shape=jax.ShapeDtypeStruct((m, n), F32),
        grid=(m // bm, n // bn),
        in_specs=[pl.BlockSpec((bm, k), lambda i, j: (i, 0)),
                  pl.BlockSpec((k, bn), lambda i, j: (0, j)), tile],
        out_specs=tile,
        compiler_params=_cparams(2),
        name="ffn_down",
    )(a, w_bf16, resid)


def _merge_kernel(yp_ref, ya_ref, ym_ref, wp_ref, wa_ref, wm_ref, gp_ref, ga_ref, gm_ref,
                  o_ref, wpb_ref, wab_ref, wmb_ref):
    @pl.when(pl.program_id(1) == 0)
    def _():
        wpb_ref[...] = wp_ref[...].astype(BF16)
        wab_ref[...] = wa_ref[...].astype(BF16)
        wmb_ref[...] = wm_ref[...].astype(BF16)

    m = jax.nn.sigmoid(gp_ref[...]) * jnp.dot(yp_ref[...], wpb_ref[...], preferred_element_type=F32)
    m = m + jax.nn.sigmoid(ga_ref[...]) * jnp.dot(ya_ref[...], wab_ref[...], preferred_element_type=F32)
    m = m + jax.nn.sigmoid(gm_ref[...]) * jnp.dot(ym_ref[...], wmb_ref[...], preferred_element_type=F32)
    o_ref[...] = m.astype(o_ref.dtype)


def _merge(gates, y_pool, y_swa, y_mem, w_bp, w_bs, w_bm, l, bm=BM_MERGE, bn=WB):
    m = gates.shape[0]
    nb = D_MODEL // bn
    act = lambda k: pl.BlockSpec((bm, k), lambda j, i: (i, 0))
    wgt = lambda k: pl.BlockSpec((None, k, bn), lambda j, i: (l, 0, j))
    gate = lambda t: pl.BlockSpec((bm, bn), lambda j, i: (i, j + t * nb))
    return pl.pallas_call(
        _merge_kernel,
        out_shape=jax.ShapeDtypeStruct((m, D_MODEL), BF16),
        grid=(nb, m // bm),
        in_specs=[act(POOL_WIDTH), act(SWA_WIDTH), act(MEM_WIDTH),
                  wgt(POOL_WIDTH), wgt(SWA_WIDTH), wgt(MEM_WIDTH),
                  gate(0), gate(1), gate(2)],
        out_specs=pl.BlockSpec((bm, bn), lambda j, i: (i, j)),
        scratch_shapes=[pltpu.VMEM((POOL_WIDTH, bn), BF16), pltpu.VMEM((SWA_WIDTH, bn), BF16),
                        pltpu.VMEM((MEM_WIDTH, bn), BF16)],
        compiler_params=_cparams(2),
        name="merge",
    )(y_pool, y_swa, y_mem, w_bp, w_bs, w_bm, gates, gates, gates)


def _pool_kernel(*refs, bb, s_len, n_hist, rc):
    u_ref, h_ref, w_ref, s_ref = refs[:4]
    o_ref, full_ref = refs[-2:]
    full_ref[:, 0:POOL_PAD, :] = h_ref[...]
    full_ref[:, POOL_PAD:, :] = u_ref[...].reshape(bb, s_len, POOL_WIDTH)
    for g, win in enumerate(POOL_WINDOWS):
        lanes = slice(g * POOL_GROUP, (g + 1) * POOL_GROUP)
        wg = w_ref[g].astype(BF16)
        sc = s_ref[:, lanes]
        for r0 in range(0, s_len, rc):
            base = POOL_PAD + r0
            cur = full_ref[:, base:base + rc, lanes]
            acc = cur
            for d in range(1, win):
                acc = acc + full_ref[:, base - d:base - d + rc, lanes]
            t = lax.broadcasted_iota(jnp.int32, (bb, rc, POOL_GROUP), 1) + (r0 + n_hist + 1)
            cnt = jnp.minimum(t, win).astype(F32)
            mixed = (acc / cnt - cur).reshape(bb * rc, POOL_GROUP).astype(BF16)
            y = jnp.dot(mixed, wg, preferred_element_type=F32) * sc
            o_ref[r0:r0 + bb * rc, lanes] = y.astype(o_ref.dtype)


def _pool(u, hist, pool_w, pool_scale, l, n_hist, n_batch, s_len, bb, row0, out_rows, prev=None):
    rows = bb * s_len
    blk0 = row0 // rows
    rc = min(s_len, 256)
    assert bb == 1 or rc == s_len
    if hist.ndim == 4:
        hspec = pl.BlockSpec((None, bb, POOL_PAD, POOL_WIDTH), lambda b: (l, b, 0, 0))
    else:
        hspec = pl.BlockSpec((bb, POOL_PAD, POOL_WIDTH), lambda b: (b, 0, 0))
    in_specs = [pl.BlockSpec((rows, POOL_WIDTH), lambda b: (blk0 + b, 0)), hspec,
                pl.BlockSpec((None, len(POOL_WINDOWS), POOL_GROUP, POOL_GROUP), lambda b: (l, 0, 0, 0)),
                pl.BlockSpec((None, 1, POOL_WIDTH), lambda b: (l, 0, 0))]
    args = [u, hist, pool_w, pool_scale]
    aliases = {}
    if prev is not None:
        in_specs.append(_any_spec())
        args.append(prev)
        aliases = {4: 0}
    return pl.pallas_call(
        functools.partial(_pool_kernel, bb=bb, s_len=s_len, n_hist=n_hist, rc=rc),
        out_shape=jax.ShapeDtypeStruct((out_rows, POOL_WIDTH), BF16),
        grid=(n_batch // bb,),
        in_specs=in_specs,
        out_specs=pl.BlockSpec((rows, POOL_WIDTH), lambda b: (blk0 + b, 0)),
        scratch_shapes=[pltpu.VMEM((bb, POOL_PAD + s_len, POOL_WIDTH), F32)],
        input_output_aliases=aliases,
        compiler_params=_cparams(1),
        name="pool",
    )(*args)


def _rel_bucket(rel):
    half = REL_BUCKETS // 2
    max_exact = half // 2
    n = jnp.abs(rel)
    nf = jnp.maximum(n, 1).astype(jnp.float32)
    large = max_exact + (jnp.log(nf / max_exact) / math.log(REL_MAX_DIST / max_exact)
                         * (half - max_exact)).astype(jnp.int32)
    large = jnp.minimum(large, half - 1)
    return jnp.where(rel > 0, half, 0) + jnp.where(n < max_exact, n, large)


def _bias_kernel(tab_ref, idx_ref, o_ref):
    h = pl.program_id(0)
    idx = idx_ref[...]
    acc = jnp.where(idx < 0, NEG, 0.0).astype(F32)
    for b in range(REL_BUCKETS):
        acc = acc + jnp.where(idx == b, tab_ref[b, h], 0.0)
    o_ref[0] = acc


def _bias_table(table, sq, n_keys, key_zero):
    band_lo = key_zero - SWA_WINDOW
    j = jnp.arange(SWA_BAND)[None, :]
    qi = jnp.arange(sq)[:, None]
    idx = _rel_bucket(j - key_zero - qi)
    idx = jnp.where((j >= band_lo) & (j < band_lo + n_keys), idx, -1).astype(jnp.int32)
    return pl.pallas_call(
        _bias_kernel,
        out_shape=jax.ShapeDtypeStruct((SWA_HEADS, sq, SWA_BAND), F32),
        grid=(SWA_HEADS,),
        in_specs=[pl.BlockSpec(memory_space=pltpu.SMEM),
                  pl.BlockSpec((sq, SWA_BAND), lambda h: (0, 0))],
        out_specs=pl.BlockSpec((1, sq, SWA_BAND), lambda h: (h, 0, 0)),
        compiler_params=_cparams(1),
        name="rel_bias",
    )(table, idx)


def _swa_core(q, kband, vband, b_ref, s_ref, lo, sq):
    rows = SWA_GROUP * sq
    lane_grp = lax.shift_right_logical(lax.broadcasted_iota(jnp.int32, (sq, SWA_SPAN), 1), 6)
    col_ok = None
    if lo is not None:
        col_ok = lax.broadcasted_iota(jnp.int32, (SWA_HEADS * sq, SWA_BAND), 1) >= lo
    parts = []
    for h in range(SWA_KV_HEADS):
        lanes = slice(h * SWA_SPAN, (h + 1) * SWA_SPAN)
        qs = q[:, lanes]
        qstack = jnp.concatenate(
            [jnp.where(lane_grp == g, qs, jnp.zeros_like(qs)) for g in range(SWA_GROUP)], axis=0)
        parts.append(lax.dot_general(qstack, kband[:, lanes], (((1,), (1,)), ((), ())),
                                     preferred_element_type=F32))
    s = jnp.concatenate(parts, axis=0) + b_ref[...].reshape(SWA_HEADS * sq, SWA_BAND)
    if col_ok is not None:
        s = jnp.where(col_ok, s, NEG)
    sk = s_ref[:, 0:1]
    m = jnp.maximum(jnp.max(s, axis=1, keepdims=True), sk)
    e = jnp.exp(s - m)
    z = jnp.sum(e, axis=1, keepdims=True) + jnp.exp(sk - m)
    p = (e / z).astype(BF16)
    outs = []
    for h in range(SWA_KV_HEADS):
        lanes = slice(h * SWA_SPAN, (h + 1) * SWA_SPAN)
        r = jnp.dot(p[h * rows:(h + 1) * rows], vband[:, lanes], preferred_element_type=F32)
        o = jnp.where(lane_grp == 0, r[0:sq], 0.0)
        for g in range(1, SWA_GROUP):
            o = o + jnp.where(lane_grp == g, r[g * sq:(g + 1) * sq], 0.0)
        outs.append(o)
    return jnp.concatenate(outs, axis=1)


def _swa_prompt_kernel(*refs, n_kb):
    q_ref = refs[0]
    k_refs = refs[1:1 + n_kb]
    v_refs = refs[1 + n_kb:1 + 2 * n_kb]
    b_ref, s_ref = refs[1 + 2 * n_kb:3 + 2 * n_kb]
    o_ref = refs[-1]
    sq = q_ref.shape[0]
    kband = jnp.concatenate([r[...] for r in k_refs], axis=0)
    vband = jnp.concatenate([r[...] for r in v_refs], axis=0)
    lo = jnp.maximum(0, (n_kb - 1 - pl.program_id(1)) * CHUNK)
    o = _swa_core(q_ref[...], kband, vband, b_ref, s_ref, lo, sq)
    o_ref[...] = o.astype(o_ref.dtype)


def _swa_prompt(q, kt, vt, bias, sink_cols, l, n_batch, n_chunk, prev):
    sq = CHUNK
    n_kb = SWA_BAND // CHUNK
    qspec = pl.BlockSpec((sq, SWA_WIDTH), lambda b, c: (b * n_chunk + c, 0))

    def band(d):
        return pl.BlockSpec((CHUNK, SWA_WIDTH),
                            lambda b, c: (b * n_chunk + jnp.maximum(c - (n_kb - 1) + d, 0), 0))

    return pl.pallas_call(
        functools.partial(_swa_prompt_kernel, n_kb=n_kb),
        out_shape=jax.ShapeDtypeStruct(prev.shape, BF16),
        grid=(n_batch, n_chunk),
        in_specs=[qspec] + [band(d) for d in range(n_kb)] * 2
        + [pl.BlockSpec((SWA_HEADS, sq, SWA_BAND), lambda b, c: (0, 0, 0)),
           pl.BlockSpec((None, SWA_HEADS * sq, 128), lambda b, c: (l, 0, 0)),
           _any_spec()],
        out_specs=qspec,
        input_output_aliases={3 + 2 * n_kb: 0},
        compiler_params=_cparams(2),
        name="swa_prompt",
    )(q, *([kt] * n_kb), *([vt] * n_kb), bias, sink_cols, prev)


def _swa_sample_kernel(q_ref, ck_ref, cv_ref, kvn_ref, e_ref, b_ref, s_ref, prev_ref, o_ref):
    del prev_ref
    sq = q_ref.shape[0]
    n_cache = ck_ref.shape[0]
    kvn = kvn_ref[...]
    zpad = jnp.zeros((SWA_BAND - n_cache - sq, SWA_KV_WIDTH), F32)
    kk = jnp.concatenate([ck_ref[...], kvn[:, :SWA_KV_WIDTH], zpad], axis=0).astype(BF16)
    vv = jnp.concatenate([cv_ref[...], kvn[:, SWA_KV_WIDTH:], zpad], axis=0).astype(BF16)
    kband = jnp.dot(kk, e_ref[...], preferred_element_type=F32).astype(BF16)
    vband = jnp.dot(vv, e_ref[...], preferred_element_type=F32).astype(BF16)
    o = _swa_core(q_ref[...], kband, vband, b_ref, s_ref, None, sq)
    o_ref[...] = o.astype(o_ref.dtype)


def _swa_sample(q, cache_k, cache_v, kv, expand, bias, sink_cols, l, row0, n_batch, sq, prev):
    blk0 = row0 // sq
    n_cache = cache_k.shape[2]
    cspec = pl.BlockSpec((None, None, n_cache, SWA_KV_WIDTH), lambda b: (l, b, 0, 0))
    return pl.pallas_call(
        _swa_sample_kernel,
        out_shape=jax.ShapeDtypeStruct(prev.shape, BF16),
        grid=(n_batch,),
        in_specs=[pl.BlockSpec((sq, SWA_WIDTH), lambda b: (blk0 + b, 0)), cspec, cspec,
                  pl.BlockSpec((sq, 2 * SWA_KV_WIDTH), lambda b: (blk0 + b, 0)),
                  pl.BlockSpec((SWA_KV_WIDTH, SWA_WIDTH), lambda b: (0, 0)),
                  pl.BlockSpec((SWA_HEADS, sq, SWA_BAND), lambda b: (0, 0, 0)),
                  pl.BlockSpec((None, SWA_HEADS * sq, 128), lambda b: (l, 0, 0)),
                  _any_spec()],
        out_specs=pl.BlockSpec((sq, SWA_WIDTH), lambda b: (blk0 + b, 0)),
        input_output_aliases={7: 0},
        compiler_params=_cparams(1),
        name="swa_sample",
    )(q, cache_k, cache_v, kv, expand, bias, sink_cols, prev)


def _mem_kernel(*refs):
    q_ref, k_ref, v_ref = refs[:3]
    o_ref = refs[-1]
    for h in range(MEM_HEADS):
        lanes = slice(h * MEM_HEAD_DIM, (h + 1) * MEM_HEAD_DIM)
        kh = k_ref[:, lanes].astype(BF16)
        vh = v_ref[:, lanes].astype(BF16)
        s = lax.dot_general(q_ref[:, lanes], kh, (((1,), (1,)), ((), ())),
                            preferred_element_type=F32)
        m = jnp.max(s, axis=1, keepdims=True)
        e = jnp.exp(s - m)
        p = (e / jnp.sum(e, axis=1, keepdims=True)).astype(BF16)
        o_ref[:, lanes] = jnp.dot(p, vh, preferred_element_type=F32).astype(o_ref.dtype)


def _mem_attend(mq, kspec, vspec, k, v, row0, n_batch, s_len, bq, prev=None):
    per = s_len // bq
    blk0 = row0 // bq
    qspec = pl.BlockSpec((bq, MEM_WIDTH), lambda b, s: (blk0 + b * per + s, 0))
    in_specs = [qspec, kspec, vspec]
    args = [mq, k, v]
    aliases = {}
    if prev is not None:
        in_specs.append(_any_spec())
        args.append(prev)
        aliases = {3: 0}
    return pl.pallas_call(
        _mem_kernel,
        out_shape=jax.ShapeDtypeStruct(mq.shape, BF16),
        grid=(n_batch, per),
        in_specs=in_specs,
        out_specs=qspec,
        input_output_aliases=aliases,
        compiler_params=_cparams(2),
        name="mem_attend",
    )(*args)


def _head_expansion():
    src = jnp.arange(SWA_KV_WIDTH)
    dst = jnp.arange(SWA_WIDTH)
    dst_src = (dst // SWA_SPAN) * SWA_HEAD_DIM + dst % SWA_HEAD_DIM
    return (src[:, None] == dst_src[None, :]).astype(BF16)


def _ffn_half(h, x, w_in, w_out, l):
    hh, w_out_bf16 = _ffn_up(h, w_in, w_out, l)
    return _ffn_down(hh, w_out_bf16, x)


def kernel(x_prompt, x_sample, cache_swa_k, cache_swa_v, state_pool, cache_mem_k, cache_mem_v,
           mem_prompt, ffn_a_norm, ffn_a_w_in, ffn_a_w_out, mix_norm, mem_norm, w_in, pool_w,
           pool_scale, swa_sinks, rel_bias, w_mem_kv, w_branch_pool, w_branch_swa, w_branch_mem,
           w_out, ffn_b_norm, ffn_b_w_in, ffn_b_w_out, final_norm):
    bp, sp, _ = x_prompt.shape
    bs, ss, _ = x_sample.shape
    depth = w_in.shape[0]
    tp = bp * sp
    ts = bs * ss
    t_all = tp + ts
    n_chunk = sp // CHUNK
    n_cache = cache_swa_k.shape[2]
    kvh = (SWA_KV_HEADS, SWA_HEAD_DIM)

    mem_x = mem_prompt.reshape(bp * N_MEM, D_MODEL)

    bias_p = _bias_table(rel_bias, CHUNK, SWA_BAND - CHUNK, SWA_BAND - CHUNK)
    bias_s = _bias_table(rel_bias, ss, n_cache + ss, n_cache)
    expand = _head_expansion()
    hist_p = jnp.zeros((bp, POOL_PAD, POOL_WIDTH), F32)
    hist_s = jnp.pad(state_pool, ((0, 0), (0, 0), (POOL_PAD - POOL_HIST, 0), (0, 0)))
    sinks = swa_sinks.astype(F32).reshape(depth, SWA_HEADS, 1, 1)
    sink_p = jnp.broadcast_to(sinks, (depth, SWA_HEADS, CHUNK, 128)).reshape(depth, SWA_HEADS * CHUNK, 128)
    sink_s = jnp.broadcast_to(sinks, (depth, SWA_HEADS, ss, 128)).reshape(depth, SWA_HEADS * ss, 128)
    norm3 = lambda g: g.reshape(depth, 1, D_MODEL)
    ffn_a_g, mix_g, mem_g, ffn_b_g = norm3(ffn_a_norm), norm3(mix_norm), norm3(mem_norm), norm3(ffn_b_norm)
    pool_s3 = pool_scale.reshape(depth, 1, POOL_WIDTH)
    cache_k4 = cache_swa_k.reshape(depth, bs, n_cache, SWA_KV_WIDTH)
    cache_v4 = cache_swa_v.reshape(depth, bs, n_cache, SWA_KV_WIDTH)
    cache_mk3 = cache_mem_k.reshape(depth, bs * N_MEM, MEM_WIDTH).astype(BF16)
    cache_mv3 = cache_mem_v.reshape(depth, bs * N_MEM, MEM_WIDTH).astype(BF16)

    y_pool = jnp.zeros((t_all, POOL_WIDTH), BF16)
    y_swa = jnp.zeros((t_all, SWA_WIDTH), BF16)
    y_mem = jnp.zeros((t_all, MEM_WIDTH), BF16)

    p_k, p_v, p_pool, p_mk, p_mv, s_k, s_v, s_pool = ([] for _ in range(8))
    h, x = _rmsnorm_join(x_prompt.reshape(tp, D_MODEL), x_sample.reshape(ts, D_MODEL), ffn_a_g, 0)
    for l in range(depth):
        if l > 0:
            h = _rmsnorm(x, (ffn_a_g, l), BF16)
        x = _ffn_half(h, x, ffn_a_w_in, ffn_a_w_out, l)

        h = _rmsnorm(x, (mix_g, l), BF16)
        u = _proj(h, w_in, l, OFF_U, POOL_WIDTH, F32)
        q = _proj(h, w_in, l, OFF_Q, SWA_WIDTH, BF16, scale=SWA_HEAD_DIM ** -0.5)
        kv, kt, vt = _proj_kv(h, w_in, l, expand)
        mq = _proj(h, w_in, l, OFF_MQ, MEM_WIDTH, BF16, scale=MEM_HEAD_DIM ** -0.5)
        gates = _proj_staged(h, w_in, l, OFF_GATE, 3 * D_MODEL, F32)

        y_pool = _pool(u, hist_p, pool_w, pool_s3, l, 0, bp, sp, 1, 0, t_all, prev=y_pool)
        y_pool = _pool(u, hist_s, pool_w, pool_s3, l, POOL_HIST, bs, ss, bs, tp, t_all, prev=y_pool)
        u_s = u[tp:].reshape(bs, ss, POOL_WIDTH)
        p_pool.append(u[:tp].reshape(bp, sp, POOL_WIDTH)[:, sp - POOL_HIST:])
        s_pool.append(jnp.concatenate([state_pool[l], u_s], axis=1)[:, ss:])

        y_swa = _swa_prompt(q, kt, vt, bias_p, sink_p, l, bp, n_chunk, y_swa)
        y_swa = _swa_sample(q, cache_k4, cache_v4, kv, expand, bias_s, sink_s, l, tp, bs, ss, y_swa)
        k_new, v_new = kv[:, :SWA_KV_WIDTH], kv[:, SWA_KV_WIDTH:]
        p_k.append(k_new[:tp].reshape(bp, sp, *kvh)[:, sp - SWA_WINDOW:])
        p_v.append(v_new[:tp].reshape(bp, sp, *kvh)[:, sp - SWA_WINDOW:])
        s_k.append(jnp.concatenate([cache_k4[l][:, ss:], k_new[tp:].reshape(bs, ss, SWA_KV_WIDTH)],
                                   axis=1).reshape(bs, n_cache, *kvh))
        s_v.append(jnp.concatenate([cache_v4[l][:, ss:], v_new[tp:].reshape(bs, ss, SWA_KV_WIDTH)],
                                   axis=1).reshape(bs, n_cache, *kvh))

        mkv = _proj(_rmsnorm(mem_x, (mem_g, l), BF16, bm=256), w_mem_kv, l, 0, 2 * MEM_WIDTH, F32, bm=256)
        y_mem = _mem_attend(mq, pl.BlockSpec((N_MEM, MEM_WIDTH), lambda b, s: (b, 0)),
                            pl.BlockSpec((N_MEM, MEM_WIDTH), lambda b, s: (b, 1)),
                            mkv, mkv, 0, bp, sp, 512, prev=y_mem)
        cache_spec = pl.BlockSpec((None, N_MEM, MEM_WIDTH), lambda b, s: (l, b, 0))
        y_mem = _mem_attend(mq, cache_spec, cache_spec, cache_mk3, cache_mv3, tp, bs, ss, ss,
                            prev=y_mem)
        p_mk.append(mkv[:, :MEM_WIDTH].reshape(bp, N_MEM, MEM_HEADS, MEM_HEAD_DIM))
        p_mv.append(mkv[:, MEM_WIDTH:].reshape(bp, N_MEM, MEM_HEADS, MEM_HEAD_DIM))

        m = _merge(gates, y_pool, y_swa, y_mem, w_branch_pool, w_branch_swa, w_branch_mem, l)
        x = _proj_resid(m, w_out, l, x)

        x = _ffn_half(_rmsnorm(x, (ffn_b_g, l), BF16), x, ffn_b_w_in, ffn_b_w_out, l)

    fin = (final_norm.reshape(1, D_MODEL), None)
    y_p = _rmsnorm(x, fin, F32, bm=256, row0=0, rows=tp)
    y_s = _rmsnorm(x, fin, F32, bm=256, row0=tp, rows=ts)
    return (y_p.reshape(bp, sp, D_MODEL), y_s.reshape(bs, ss, D_MODEL),
            jnp.stack(p_k), jnp.stack(p_v), jnp.stack(p_pool), jnp.stack(p_mk), jnp.stack(p_mv),
            jnp.stack(s_k), jnp.stack(s_v), jnp.stack(s_pool))
```

```python
import functools
import math

import jax
import jax.numpy as jnp
from jax import lax
from jax.experimental import pallas as pl
from jax.experimental.pallas import tpu as pltpu

F32 = jnp.float32
BF16 = jnp.bfloat16

D_MODEL = 4096
D_FF = 11008
CHUNK = 64
EPS = 1e-6
POOL_WINDOWS = (2, 4, 8, 16)
POOL_GROUP = 256
POOL_WIDTH = 1024
POOL_HIST = 15
POOL_PAD = 16
SWA_WINDOW = 128
SWA_HEAD_DIM = 64
SWA_HEADS = 32
SWA_KV_HEADS = 4
SWA_GROUP = 8
SWA_WIDTH = 2048
SWA_KV_WIDTH = 256
SWA_SPAN = SWA_GROUP * SWA_HEAD_DIM
SWA_BAND = 256
N_MEM = 256
MEM_HEADS = 4
MEM_WIDTH = 1024
MEM_HEAD_DIM = 256
REL_BUCKETS = 32
REL_MAX_DIST = 128
NEG = -1e30

OFF_U = 0
OFF_Q = OFF_U + POOL_WIDTH
OFF_KV = OFF_Q + SWA_WIDTH
OFF_MQ = OFF_KV + 2 * SWA_KV_WIDTH
OFF_GATE = OFF_MQ + MEM_WIDTH

VMEM_LIMIT_BYTES = 58 * 1024 * 1024
VMEM_LIMIT_STAGED_BYTES = 62 * 1024 * 1024
BM = 1088
WB = 512
NW = 1
BM_MERGE = 544
FB = 256
BM_DOWN = 544
BN_DOWN = 512
LANES = 128
CAST_ROWS = 512


def _cparams(n_axes, vmem_limit_bytes=VMEM_LIMIT_BYTES):
    return pltpu.CompilerParams(
        dimension_semantics=("arbitrary",) * n_axes,
        vmem_limit_bytes=vmem_limit_bytes,
    )


def _any_spec():
    return pl.BlockSpec(memory_space=pl.ANY)


def _rms_kernel(x_ref, g_ref, o_ref):
    x = x_ref[...]
    ms = jnp.mean(x * x, axis=-1, keepdims=True)
    o_ref[...] = (x * lax.rsqrt(ms + EPS) * g_ref[...]).astype(o_ref.dtype)


def _rmsnorm(x, g, out_dtype, bm=544, row0=0, rows=None):
    d = x.shape[1]
    rows = x.shape[0] if rows is None else rows
    blk0 = row0 // bm
    g_arr, l = g
    if l is None:
        gspec = pl.BlockSpec((1, d), lambda i: (0, 0))
    else:
        gspec = pl.BlockSpec((None, 1, d), lambda i: (l, 0, 0))
    return pl.pallas_call(
        _rms_kernel,
        out_shape=jax.ShapeDtypeStruct((rows, d), out_dtype),
        grid=(rows // bm,),
        in_specs=[pl.BlockSpec((bm, d), lambda i: (blk0 + i, 0)), gspec],
        out_specs=pl.BlockSpec((bm, d), lambda i: (i, 0)),
        compiler_params=_cparams(1),
        name="rmsnorm",
    )(x, g_arr)


def _lane_partial_sumsq(x):
    sq = x * x
    part = sq[:, 0:LANES]
    for c0 in range(LANES, x.shape[1], LANES):
        part = part + sq[:, c0:c0 + LANES]
    return part


def _join_kernel(xp_ref, xs_ref, g_ref, hb_ref, x_ref, ss_ref, *, n_first):
    def emit(src_ref):
        x = src_ref[...]
        hb_ref[...] = (x * g_ref[...]).astype(hb_ref.dtype)
        x_ref[...] = x
        ss_ref[...] = _lane_partial_sumsq(x)

    i = pl.program_id(0)

    @pl.when(i < n_first)
    def _():
        emit(xp_ref)

    @pl.when(i >= n_first)
    def _():
        emit(xs_ref)


def _join(xp, xs, g, l, bm=256):
    d = xp.shape[1]
    n_first = xp.shape[0] // bm
    n_second = xs.shape[0] // bm
    rows = xp.shape[0] + xs.shape[0]
    out = pl.BlockSpec((bm, d), lambda i: (i, 0))
    return pl.pallas_call(
        functools.partial(_join_kernel, n_first=n_first),
        out_shape=(jax.ShapeDtypeStruct((rows, d), BF16), jax.ShapeDtypeStruct((rows, d), F32),
                   jax.ShapeDtypeStruct((1, rows, LANES), F32)),
        grid=(n_first + n_second,),
        in_specs=[pl.BlockSpec((bm, d), lambda i: (jnp.minimum(i, n_first - 1), 0)),
                  pl.BlockSpec((bm, d), lambda i: (jnp.maximum(i - n_first, 0), 0)),
                  pl.BlockSpec((None, 1, d), lambda i: (l, 0, 0))],
        out_specs=(out, out, pl.BlockSpec((None, bm, LANES), lambda i: (0, i, 0))),
        compiler_params=_cparams(1),
        name="join",
    )(xp, xs, g)


def _row_scale_kernel(ss_ref, r_ref):
    tot = jnp.sum(ss_ref[...], axis=0)
    ms = jnp.sum(tot, axis=1, keepdims=True) * (1.0 / D_MODEL)
    r_ref[...] = jnp.broadcast_to(lax.rsqrt(ms + EPS), r_ref.shape)


def _row_scale(ss, bm=BM):
    p, rows, _ = ss.shape
    return pl.pallas_call(
        _row_scale_kernel,
        out_shape=jax.ShapeDtypeStruct((rows, LANES), F32),
        grid=(rows // bm,),
        in_specs=[pl.BlockSpec((p, bm, LANES), lambda i: (0, i, 0))],
        out_specs=pl.BlockSpec((bm, LANES), lambda i: (i, 0)),
        compiler_params=_cparams(1),
        name="row_scale",
    )(ss)


def _stage_weights(w_refs, wb_ref):
    @pl.when(pl.program_id(1) == 0)
    def _():
        wd = w_refs[0].shape[1]
        for t, w in enumerate(w_refs):
            wb_ref[:, t * wd:(t + 1) * wd] = w[...].astype(BF16)


def _wspecs(k, wb, l, first, nw):
    return [pl.BlockSpec((None, k, wb), functools.partial(lambda j, i, t: (l, 0, first + nw * j + t), t=t))
            for t in range(nw)]


def _rspec(bm):
    return pl.BlockSpec((bm, LANES), lambda j, i: (i, 0))


def _proj_kernel(*refs, nw, scale, scaled_rows):
    a_ref, w_refs = refs[0], refs[1:1 + nw]
    o_ref, wb_ref = refs[-2:]
    _stage_weights(w_refs, wb_ref)
    acc = jnp.dot(a_ref[...], wb_ref[...], preferred_element_type=F32)
    if scaled_rows:
        acc = acc * refs[1 + nw][:, 0:1]
    if scale != 1.0:
        acc = acc * scale
    o_ref[...] = acc.astype(o_ref.dtype)


def _proj(a, w, l, col0, ncols, out_dtype, rscale=None, scale=1.0, bm=BM, wb=WB, nw=NW):
    m, k = a.shape
    bn = wb * nw
    extra_specs, extra = ([], []) if rscale is None else ([_rspec(bm)], [rscale])
    return pl.pallas_call(
        functools.partial(_proj_kernel, nw=nw, scale=scale, scaled_rows=rscale is not None),
        out_shape=jax.ShapeDtypeStruct((m, ncols), out_dtype),
        grid=(ncols // bn, m // bm),
        in_specs=[pl.BlockSpec((bm, k), lambda j, i: (i, 0))] + _wspecs(k, wb, l, col0 // wb, nw)
        + extra_specs,
        out_specs=pl.BlockSpec((bm, bn), lambda j, i: (i, j)),
        scratch_shapes=[pltpu.VMEM((k, bn), BF16)],
        compiler_params=_cparams(2),
        name="proj",
    )(a, *([w] * nw), *extra)


def _proj_kv_kernel(a_ref, w_ref, e_ref, r_ref, kv_ref, kt_ref, vt_ref, wb_ref):
    _stage_weights((w_ref,), wb_ref)
    acc = jnp.dot(a_ref[...], wb_ref[...], preferred_element_type=F32) * r_ref[:, 0:1]
    kv_ref[...] = acc
    kv = acc.astype(BF16)
    kt_ref[...] = jnp.dot(kv[:, :SWA_KV_WIDTH], e_ref[...], preferred_element_type=F32).astype(BF16)
    vt_ref[...] = jnp.dot(kv[:, SWA_KV_WIDTH:], e_ref[...], preferred_element_type=F32).astype(BF16)


def _proj_kv(a, w, l, expand, rscale, bm=BM // 2):
    m, k = a.shape
    n = 2 * SWA_KV_WIDTH
    row = lambda width: pl.BlockSpec((bm, width), lambda j, i: (i, 0))
    return pl.pallas_call(
        _proj_kv_kernel,
        out_shape=(jax.ShapeDtypeStruct((m, n), F32),
                   jax.ShapeDtypeStruct((m, SWA_WIDTH), BF16),
                   jax.ShapeDtypeStruct((m, SWA_WIDTH), BF16)),
        grid=(1, m // bm),
        in_specs=[row(k)] + _wspecs(k, n, l, OFF_KV // n, 1)
        + [pl.BlockSpec((SWA_KV_WIDTH, SWA_WIDTH), lambda j, i: (0, 0)), _rspec(bm)],
        out_specs=(row(n), row(SWA_WIDTH), row(SWA_WIDTH)),
        scratch_shapes=[pltpu.VMEM((k, n), BF16)],
        compiler_params=_cparams(2),
        name="proj_kv",
    )(a, w, expand, rscale)


def _emit_stream(x_new, g_ref, o_ref, hb_ref):
    o_ref[...] = x_new
    hb_ref[...] = (x_new * g_ref[...]).astype(hb_ref.dtype)


def _proj_resid_kernel(*refs, nw):
    a_ref, w_refs = refs[0], refs[1:1 + nw]
    r_ref, g_ref, o_ref, hb_ref, ss_ref, wb_ref = refs[1 + nw:]
    _stage_weights(w_refs, wb_ref)
    x_new = r_ref[...] + jnp.dot(a_ref[...], wb_ref[...], preferred_element_type=F32)
    _emit_stream(x_new, g_ref, o_ref, hb_ref)
    ss_ref[...] = _lane_partial_sumsq(x_new)


def _proj_resid(a, w, l, resid, g, gl, bm=BM, wb=WB, nw=NW):
    m, k = a.shape
    n = w.shape[2]
    bn = wb * nw
    nj = n // bn
    tile = pl.BlockSpec((bm, bn), lambda j, i: (i, j))
    return pl.pallas_call(
        functools.partial(_proj_resid_kernel, nw=nw),
        out_shape=(jax.ShapeDtypeStruct((m, n), F32), jax.ShapeDtypeStruct((m, n), BF16),
                   jax.ShapeDtypeStruct((nj, m, LANES), F32)),
        grid=(nj, m // bm),
        in_specs=[pl.BlockSpec((bm, k), lambda j, i: (i, 0))] + _wspecs(k, wb, l, 0, nw)
        + [tile, pl.BlockSpec((None, 1, bn), lambda j, i: (gl, 0, j))],
        out_specs=(tile, tile, pl.BlockSpec((None, bm, LANES), lambda j, i: (j, i, 0))),
        scratch_shapes=[pltpu.VMEM((k, bn), BF16)],
        compiler_params=_cparams(2),
        name="proj_resid",
    )(a, *([w] * nw), resid, g)


def _staged_weights(i, j, nj, start_copies, wait_copies, stage_ref, wb_ref):
    @pl.when(i == 0)
    def _():
        @pl.when(j == 0)
        def _():
            start_copies(0)

        wait_copies(j)
        for r0 in range(0, wb_ref.shape[0], CAST_ROWS):
            wb_ref[r0:r0 + CAST_ROWS, :] = stage_ref[r0:r0 + CAST_ROWS, :].astype(BF16)

        @pl.when(j + 1 < nj)
        def _():
            start_copies(j + 1)


def _proj_staged_kernel(a_ref, w_hbm, r_ref, o_ref, stage_ref, wb_ref, sem, *, l, nj, col0, scale):
    j = pl.program_id(0)
    i = pl.program_id(1)
    bn = o_ref.shape[1]

    def copy(jj):
        col = pl.multiple_of(col0 + jj * bn, WB)
        return pltpu.make_async_copy(w_hbm.at[l, :, pl.ds(col, bn)], stage_ref, sem.at[0])

    _staged_weights(i, j, nj, lambda jj: copy(jj).start(), lambda jj: copy(jj).wait(),
                    stage_ref, wb_ref)
    rc = a_ref.shape[0] // 4
    for r0 in range(0, a_ref.shape[0], rc):
        acc = jnp.dot(a_ref[r0:r0 + rc, :], wb_ref[...], preferred_element_type=F32)
        acc = acc * r_ref[r0:r0 + rc, 0:1]
        if scale != 1.0:
            acc = acc * scale
        o_ref[r0:r0 + rc, :] = acc.astype(o_ref.dtype)


def _proj_staged(a, w, l, col0, ncols, out_dtype, rscale, scale=1.0, bm=BM, bn=2 * WB):
    m, k = a.shape
    nj = ncols // bn
    assert col0 % WB == 0 and ncols % bn == 0
    return pl.pallas_call(
        functools.partial(_proj_staged_kernel, l=l, nj=nj, col0=col0, scale=scale),
        out_shape=jax.ShapeDtypeStruct((m, ncols), out_dtype),
        grid=(nj, m // bm),
        in_specs=[pl.BlockSpec((bm, k), lambda j, i: (i, 0)), _any_spec(), _rspec(bm)],
        out_specs=pl.BlockSpec((bm, bn), lambda j, i: (i, j)),
        scratch_shapes=[pltpu.VMEM((k, bn), F32), pltpu.VMEM((k, bn), BF16),
                        pltpu.SemaphoreType.DMA((1,))],
        compiler_params=_cparams(2, VMEM_LIMIT_STAGED_BYTES),
        name="proj_staged",
    )(a, w, rscale)


def _swiglu_kernel(a_ref, w_hbm, wo_ref, r_ref, o_ref, wob_ref, stage_ref, wb_ref, sem,
                   *, l, nj, fn, tail):
    j = pl.program_id(0)
    i = pl.program_id(1)

    def copies(jj, width):
        col = pl.multiple_of(jj * fn, fn)
        return (pltpu.make_async_copy(w_hbm.at[l, :, pl.ds(col, width)],
                                      stage_ref.at[:, :width], sem.at[0]),
                pltpu.make_async_copy(w_hbm.at[l, :, pl.ds(D_FF + col, width)],
                                      stage_ref.at[:, fn:fn + width], sem.at[1]))

    def each(jj, method):
        @pl.when(jj < nj - 1)
        def _():
            for c in copies(jj, fn):
                getattr(c, method)()

        @pl.when(jj == nj - 1)
        def _():
            for c in copies(jj, tail):
                getattr(c, method)()

    _staged_weights(i, j, nj, lambda jj: each(jj, "start"), lambda jj: each(jj, "wait"),
                    stage_ref, wb_ref)
    wob_ref[...] = wo_ref[...].astype(BF16)

    def swiglu(width):
        rc = a_ref.shape[0] // 4
        for r0 in range(0, a_ref.shape[0], rc):
            a = a_ref[r0:r0 + rc, :]
            rs = r_ref[r0:r0 + rc, 0:1]
            if width == fn:
                acc = jnp.dot(a, wb_ref[...], preferred_element_type=F32) * rs
                gate, up = acc[:, :fn], acc[:, fn:]
            else:
                gate = jnp.dot(a, wb_ref[:, :width], preferred_element_type=F32) * rs
                up = jnp.dot(a, wb_ref[:, fn:fn + width], preferred_element_type=F32) * rs
                o_ref[r0:r0 + rc, width:] = jnp.zeros((rc, fn - width), o_ref.dtype)
            o_ref[r0:r0 + rc, :width] = (gate * jax.nn.sigmoid(gate) * up).astype(o_ref.dtype)

    @pl.when(j < nj - 1)
    def _():
        swiglu(fn)

    @pl.when(j == nj - 1)
    def _():
        swiglu(tail)


def _ffn_up(a, w_in, w_out, l, rscale, bm=BM, fn=2 * FB):
    m, k = a.shape
    nj = pl.cdiv(D_FF, fn)
    tail = D_FF - (nj - 1) * fn
    ni = m // bm
    n_out = w_out.shape[2]
    oc = n_out // ni
    return pl.pallas_call(
        functools.partial(_swiglu_kernel, l=l, nj=nj, fn=fn, tail=tail),
        out_shape=(jax.ShapeDtypeStruct((m, nj * fn), BF16),
                   jax.ShapeDtypeStruct((D_FF, n_out), BF16)),
        grid=(nj, ni),
        in_specs=[pl.BlockSpec((bm, k), lambda j, i: (i, 0)),
                  _any_spec(),
                  pl.BlockSpec((None, fn, oc), lambda j, i: (l, j, i)),
                  _rspec(bm)],
        out_specs=(pl.BlockSpec((bm, fn), lambda j, i: (i, j)),
                   pl.BlockSpec((fn, oc), lambda j, i: (j, i))),
        scratch_shapes=[pltpu.VMEM((k, 2 * fn), F32), pltpu.VMEM((k, 2 * fn), BF16),
                        pltpu.SemaphoreType.DMA((2,))],
        compiler_params=_cparams(2, VMEM_LIMIT_STAGED_BYTES),
        name="ffn_up",
    )(a, w_in, w_out, rscale)


def _ffn_down_kernel(a_ref, w_ref, r_ref, g_ref, o_ref, hb_ref, ss_ref):
    x_new = r_ref[...] + 0.5 * jnp.dot(a_ref[...], w_ref[...], preferred_element_type=F32)
    _emit_stream(x_new, g_ref, o_ref, hb_ref)
    part = _lane_partial_sumsq(x_new)
    j = pl.program_id(1)

    @pl.when(j == 0)
    def _():
        ss_ref[...] = part

    @pl.when(j > 0)
    def _():
        ss_ref[...] += part


def _ffn_down(a, w_bf16, resid, g, gl, bm=BM_DOWN, bn=BN_DOWN):
    m = a.shape[0]
    k, n = w_bf16.shape
    tile = pl.BlockSpec((bm, bn), lambda i, j: (i, j))
    return pl.pallas_call(
        _ffn_down_kernel,
        out_shape=(jax.ShapeDtypeStruct((m, n), F32), jax.ShapeDtypeStruct((m, n), BF16),
                   jax.ShapeDtypeStruct((1, m, LANES), F32)),
        grid=(m // bm, n // bn),
        in_specs=[pl.BlockSpec((bm, k), lambda i, j: (i, 0)),
                  pl.BlockSpec((k, bn), lambda i, j: (0, j)), tile,
                  pl.BlockSpec((None, 1, bn), lambda i, j: (gl, 0, j))],
        out_specs=(tile, tile, pl.BlockSpec((None, bm, LANES), lambda i, j: (0, i, 0))),
        compiler_params=_cparams(2),
        name="ffn_down",
    )(a, w_bf16, resid, g)


def _merge_kernel(yp_ref, ya_ref, ym_ref, wp_ref, wa_ref, wm_ref, gp_ref, ga_ref, gm_ref,
                  o_ref, wpb_ref, wab_ref, wmb_ref):
    @pl.when(pl.program_id(1) == 0)
    def _():
        wpb_ref[...] = wp_ref[...].astype(BF16)
        wab_ref[...] = wa_ref[...].astype(BF16)
        wmb_ref[...] = wm_ref[...].astype(BF16)

    m = jax.nn.sigmoid(gp_ref[...]) * jnp.dot(yp_ref[...], wpb_ref[...], preferred_element_type=F32)
    m = m + jax.nn.sigmoid(ga_ref[...]) * jnp.dot(ya_ref[...], wab_ref[...], preferred_element_type=F32)
    m = m + jax.nn.sigmoid(gm_ref[...]) * jnp.dot(ym_ref[...], wmb_ref[...], preferred_element_type=F32)
    o_ref[...] = m.astype(o_ref.dtype)


def _merge(gates, y_pool, y_swa, y_mem, w_bp, w_bs, w_bm, l, bm=BM_MERGE, bn=WB):
    m = gates.shape[0]
    nb = D_MODEL // bn
    act = lambda k: pl.BlockSpec((bm, k), lambda j, i: (i, 0))
    wgt = lambda k: pl.BlockSpec((None, k, bn), lambda j, i: (l, 0, j))
    gate = lambda t: pl.BlockSpec((bm, bn), lambda j, i: (i, j + t * nb))
    return pl.pallas_call(
        _merge_kernel,
        out_shape=jax.ShapeDtypeStruct((m, D_MODEL), BF16),
        grid=(nb, m // bm),
        in_specs=[act(POOL_WIDTH), act(SWA_WIDTH), act(MEM_WIDTH),
                  wgt(POOL_WIDTH), wgt(SWA_WIDTH), wgt(MEM_WIDTH),
                  gate(0), gate(1), gate(2)],
        out_specs=pl.BlockSpec((bm, bn), lambda j, i: (i, j)),
        scratch_shapes=[pltpu.VMEM((POOL_WIDTH, bn), BF16), pltpu.VMEM((SWA_WIDTH, bn), BF16),
                        pltpu.VMEM((MEM_WIDTH, bn), BF16)],
        compiler_params=_cparams(2),
        name="merge",
    )(y_pool, y_swa, y_mem, w_bp, w_bs, w_bm, gates, gates, gates)


def _pool_kernel(*refs, bb, s_len, n_hist, rc):
    u_ref, h_ref, w_ref, s_ref = refs[:4]
    o_ref, full_ref = refs[-2:]
    full_ref[:, 0:POOL_PAD, :] = h_ref[...]
    full_ref[:, POOL_PAD:, :] = u_ref[...].reshape(bb, s_len, POOL_WIDTH)
    for g, win in enumerate(POOL_WINDOWS):
        lanes = slice(g * POOL_GROUP, (g + 1) * POOL_GROUP)
        wg = w_ref[g].astype(BF16)
        sc = s_ref[:, lanes]
        for r0 in range(0, s_len, rc):
            base = POOL_PAD + r0
            cur = full_ref[:, base:base + rc, lanes]
            acc = cur
            for d in range(1, win):
                acc = acc + full_ref[:, base - d:base - d + rc, lanes]
            t = lax.broadcasted_iota(jnp.int32, (bb, rc, POOL_GROUP), 1) + (r0 + n_hist + 1)
            cnt = jnp.minimum(t, win).astype(F32)
            mixed = (acc / cnt - cur).reshape(bb * rc, POOL_GROUP).astype(BF16)
            y = jnp.dot(mixed, wg, preferred_element_type=F32) * sc
            o_ref[r0:r0 + bb * rc, lanes] = y.astype(o_ref.dtype)


def _pool(u, hist, pool_w, pool_scale, l, n_hist, n_batch, s_len, bb, row0, out_rows, prev=None):
    rows = bb * s_len
    blk0 = row0 // rows
    rc = min(s_len, 256)
    assert bb == 1 or rc == s_len
    if hist.ndim == 4:
        hspec = pl.BlockSpec((None, bb, POOL_PAD, POOL_WIDTH), lambda b: (l, b, 0, 0))
    else:
        hspec = pl.BlockSpec((bb, POOL_PAD, POOL_WIDTH), lambda b: (b, 0, 0))
    in_specs = [pl.BlockSpec((rows, POOL_WIDTH), lambda b: (blk0 + b, 0)), hspec,
                pl.BlockSpec((None, len(POOL_WINDOWS), POOL_GROUP, POOL_GROUP), lambda b: (l, 0, 0, 0)),
                pl.BlockSpec((None, 1, POOL_WIDTH), lambda b: (l, 0, 0))]
    args = [u, hist, pool_w, pool_scale]
    aliases = {}
    if prev is not None:
        in_specs.append(_any_spec())
        args.append(prev)
        aliases = {4: 0}
    return pl.pallas_call(
        functools.partial(_pool_kernel, bb=bb, s_len=s_len, n_hist=n_hist, rc=rc),
        out_shape=jax.ShapeDtypeStruct((out_rows, POOL_WIDTH), BF16),
        grid=(n_batch // bb,),
        in_specs=in_specs,
        out_specs=pl.BlockSpec((rows, POOL_WIDTH), lambda b: (blk0 + b, 0)),
        scratch_shapes=[pltpu.VMEM((bb, POOL_PAD + s_len, POOL_WIDTH), F32)],
        input_output_aliases=aliases,
        compiler_params=_cparams(1),
        name="pool",
    )(*args)


def _rel_bucket(rel):
    half = REL_BUCKETS // 2
    max_exact = half // 2
    n = jnp.abs(rel)
    nf = jnp.maximum(n, 1).astype(jnp.float32)
    large = max_exact + (jnp.log(nf / max_exact) / math.log(REL_MAX_DIST / max_exact)
                         * (half - max_exact)).astype(jnp.int32)
    large = jnp.minimum(large, half - 1)
    return jnp.where(rel > 0, half, 0) + jnp.where(n < max_exact, n, large)


def _bias_kernel(tab_ref, idx_ref, o_ref):
    h = pl.program_id(0)
    idx = idx_ref[...]
    acc = jnp.where(idx < 0, NEG, 0.0).astype(F32)
    for b in range(REL_BUCKETS):
        acc = acc + jnp.where(idx == b, tab_ref[b, h], 0.0)
    o_ref[0] = acc


def _bias_table(table, sq, n_keys, key_zero):
    band_lo = key_zero - SWA_WINDOW
    j = jnp.arange(SWA_BAND)[None, :]
    qi = jnp.arange(sq)[:, None]
    idx = _rel_bucket(j - key_zero - qi)
    idx = jnp.where((j >= band_lo) & (j < band_lo + n_keys), idx, -1).astype(jnp.int32)
    return pl.pallas_call(
        _bias_kernel,
        out_shape=jax.ShapeDtypeStruct((SWA_HEADS, sq, SWA_BAND), F32),
        grid=(SWA_HEADS,),
        in_specs=[pl.BlockSpec(memory_space=pltpu.SMEM),
                  pl.BlockSpec((sq, SWA_BAND), lambda h: (0, 0))],
        out_specs=pl.BlockSpec((1, sq, SWA_BAND), lambda h: (h, 0, 0)),
        compiler_params=_cparams(1),
        name="rel_bias",
    )(table, idx)


def _swa_core(q, kband, vband, b_ref, s_ref, lo, sq):
    rows = SWA_GROUP * sq
    lane_grp = lax.shift_right_logical(lax.broadcasted_iota(jnp.int32, (sq, SWA_SPAN), 1), 6)
    col_ok = None
    if lo is not None:
        col_ok = lax.broadcasted_iota(jnp.int32, (SWA_HEADS * sq, SWA_BAND), 1) >= lo
    parts = []
    for h in range(SWA_KV_HEADS):
        lanes = slice(h * SWA_SPAN, (h + 1) * SWA_SPAN)
        qs = q[:, lanes]
        qstack = jnp.concatenate(
            [jnp.where(lane_grp == g, qs, jnp.zeros_like(qs)) for g in range(SWA_GROUP)], axis=0)
        parts.append(lax.dot_general(qstack, kband[:, lanes], (((1,), (1,)), ((), ())),
                                     preferred_element_type=F32))
    s = jnp.concatenate(parts, axis=0) + b_ref[...].reshape(SWA_HEADS * sq, SWA_BAND)
    if col_ok is not None:
        s = jnp.where(col_ok, s, NEG)
    sk = s_ref[:, 0:1]
    m = jnp.maximum(jnp.max(s, axis=1, keepdims=True), sk)
    e = jnp.exp(s - m)
    z = jnp.sum(e, axis=1, keepdims=True) + jnp.exp(sk - m)
    p = (e / z).astype(BF16)
    outs = []
    for h in range(SWA_KV_HEADS):
        lanes = slice(h * SWA_SPAN, (h + 1) * SWA_SPAN)
        r = jnp.dot(p[h * rows:(h + 1) * rows], vband[:, lanes], preferred_element_type=F32)
        o = jnp.where(lane_grp == 0, r[0:sq], 0.0)
        for g in range(1, SWA_GROUP):
            o = o + jnp.where(lane_grp == g, r[g * sq:(g + 1) * sq], 0.0)
        outs.append(o)
    return jnp.concatenate(outs, axis=1)


def _swa_prompt_kernel(*refs, n_kb):
    q_ref = refs[0]
    k_refs = refs[1:1 + n_kb]
    v_refs = refs[1 + n_kb:1 + 2 * n_kb]
    b_ref, s_ref = refs[1 + 2 * n_kb:3 + 2 * n_kb]
    o_ref = refs[-1]
    sq = q_ref.shape[0]
    kband = jnp.concatenate([r[...] for r in k_refs], axis=0)
    vband = jnp.concatenate([r[...] for r in v_refs], axis=0)
    lo = jnp.maximum(0, (n_kb - 1 - pl.program_id(1)) * CHUNK)
    o = _swa_core(q_ref[...], kband, vband, b_ref, s_ref, lo, sq)
    o_ref[...] = o.astype(o_ref.dtype)


def _swa_prompt(q, kt, vt, bias, sink_cols, l, n_batch, n_chunk, prev):
    sq = CHUNK
    n_kb = SWA_BAND // CHUNK
    qspec = pl.BlockSpec((sq, SWA_WIDTH), lambda b, c: (b * n_chunk + c, 0))

    def band(d):
        return pl.BlockSpec((CHUNK, SWA_WIDTH),
                            lambda b, c: (b * n_chunk + jnp.maximum(c - (n_kb - 1) + d, 0), 0))

    return pl.pallas_call(
        functools.partial(_swa_prompt_kernel, n_kb=n_kb),
        out_shape=jax.ShapeDtypeStruct(prev.shape, BF16),
        grid=(n_batch, n_chunk),
        in_specs=[qspec] + [band(d) for d in range(n_kb)] * 2
        + [pl.BlockSpec((SWA_HEADS, sq, SWA_BAND), lambda b, c: (0, 0, 0)),
           pl.BlockSpec((None, SWA_HEADS * sq, 128), lambda b, c: (l, 0, 0)),
           _any_spec()],
        out_specs=qspec,
        input_output_aliases={3 + 2 * n_kb: 0},
        compiler_params=_cparams(2),
        name="swa_prompt",
    )(q, *([kt] * n_kb), *([vt] * n_kb), bias, sink_cols, prev)


def _swa_sample_kernel(q_ref, ck_ref, cv_ref, kvn_ref, e_ref, b_ref, s_ref, prev_ref, o_ref):
    del prev_ref
    sq = q_ref.shape[0]
    n_cache = ck_ref.shape[0]
    kvn = kvn_ref[...]
    zpad = jnp.zeros((SWA_BAND - n_cache - sq, SWA_KV_WIDTH), F32)
    kk = jnp.concatenate([ck_ref[...], kvn[:, :SWA_KV_WIDTH], zpad], axis=0).astype(BF16)
    vv = jnp.concatenate([cv_ref[...], kvn[:, SWA_KV_WIDTH:], zpad], axis=0).astype(BF16)
    kband = jnp.dot(kk, e_ref[...], preferred_element_type=F32).astype(BF16)
    vband = jnp.dot(vv, e_ref[...], preferred_element_type=F32).astype(BF16)
    o = _swa_core(q_ref[...], kband, vband, b_ref, s_ref, None, sq)
    o_ref[...] = o.astype(o_ref.dtype)


def _swa_sample(q, cache_k, cache_v, kv, expand, bias, sink_cols, l, row0, n_batch, sq, prev):
    blk0 = row0 // sq
    n_cache = cache_k.shape[2]
    cspec = pl.BlockSpec((None, None, n_cache, SWA_KV_WIDTH), lambda b: (l, b, 0, 0))
    return pl.pallas_call(
        _swa_sample_kernel,
        out_shape=jax.ShapeDtypeStruct(prev.shape, BF16),
        grid=(n_batch,),
        in_specs=[pl.BlockSpec((sq, SWA_WIDTH), lambda b: (blk0 + b, 0)), cspec, cspec,
                  pl.BlockSpec((sq, 2 * SWA_KV_WIDTH), lambda b: (blk0 + b, 0)),
                  pl.BlockSpec((SWA_KV_WIDTH, SWA_WIDTH), lambda b: (0, 0)),
                  pl.BlockSpec((SWA_HEADS, sq, SWA_BAND), lambda b: (0, 0, 0)),
                  pl.BlockSpec((None, SWA_HEADS * sq, 128), lambda b: (l, 0, 0)),
                  _any_spec()],
        out_specs=pl.BlockSpec((sq, SWA_WIDTH), lambda b: (blk0 + b, 0)),
        input_output_aliases={7: 0},
        compiler_params=_cparams(1),
        name="swa_sample",
    )(q, cache_k, cache_v, kv, expand, bias, sink_cols, prev)


def _mem_kernel(*refs):
    q_ref, k_ref, v_ref = refs[:3]
    o_ref = refs[-1]
    for h in range(MEM_HEADS):
        lanes = slice(h * MEM_HEAD_DIM, (h + 1) * MEM_HEAD_DIM)
        kh = k_ref[:, lanes].astype(BF16)
        vh = v_ref[:, lanes].astype(BF16)
        s = lax.dot_general(q_ref[:, lanes], kh, (((1,), (1,)), ((), ())),
                            preferred_element_type=F32)
        m = jnp.max(s, axis=1, keepdims=True)
        e = jnp.exp(s - m)
        p = (e / jnp.sum(e, axis=1, keepdims=True)).astype(BF16)
        o_ref[:, lanes] = jnp.dot(p, vh, preferred_element_type=F32).astype(o_ref.dtype)


def _mem_attend(mq, kspec, vspec, k, v, row0, n_batch, s_len, bq, prev=None):
    per = s_len // bq
    blk0 = row0 // bq
    qspec = pl.BlockSpec((bq, MEM_WIDTH), lambda b, s: (blk0 + b * per + s, 0))
    in_specs = [qspec, kspec, vspec]
    args = [mq, k, v]
    aliases = {}
    if prev is not None:
        in_specs.append(_any_spec())
        args.append(prev)
        aliases = {3: 0}
    return pl.pallas_call(
        _mem_kernel,
        out_shape=jax.ShapeDtypeStruct(mq.shape, BF16),
        grid=(n_batch, per),
        in_specs=in_specs,
        out_specs=qspec,
        input_output_aliases=aliases,
        compiler_params=_cparams(2),
        name="mem_attend",
    )(*args)


def _head_expansion():
    src = jnp.arange(SWA_KV_WIDTH)
    dst = jnp.arange(SWA_WIDTH)
    dst_src = (dst // SWA_SPAN) * SWA_HEAD_DIM + dst % SWA_HEAD_DIM
    return (src[:, None] == dst_src[None, :]).astype(BF16)


def _ffn_half(stream, w_in, w_out, l, g_next, gl_next):
    x, hb, ss = stream
    hh, w_out_bf16 = _ffn_up(hb, w_in, w_out, l, _row_scale(ss))
    return _ffn_down(hh, w_out_bf16, x, g_next, gl_next)


def kernel(x_prompt, x_sample, cache_swa_k, cache_swa_v, state_pool, cache_mem_k, cache_mem_v,
           mem_prompt, ffn_a_norm, ffn_a_w_in, ffn_a_w_out, mix_norm, mem_norm, w_in, pool_w,
           pool_scale, swa_sinks, rel_bias, w_mem_kv, w_branch_pool, w_branch_swa, w_branch_mem,
           w_out, ffn_b_norm, ffn_b_w_in, ffn_b_w_out, final_norm):
    bp, sp, _ = x_prompt.shape
    bs, ss, _ = x_sample.shape
    depth = w_in.shape[0]
    tp = bp * sp
    ts = bs * ss
    t_all = tp + ts
    n_chunk = sp // CHUNK
    n_cache = cache_swa_k.shape[2]
    kvh = (SWA_KV_HEADS, SWA_HEAD_DIM)

    mem_x = mem_prompt.reshape(bp * N_MEM, D_MODEL)

    bias_p = _bias_table(rel_bias, CHUNK, SWA_BAND - CHUNK, SWA_BAND - CHUNK)
    bias_s = _bias_table(rel_bias, ss, n_cache + ss, n_cache)
    expand = _head_expansion()
    hist_p = jnp.zeros((bp, POOL_PAD, POOL_WIDTH), F32)
    hist_s = jnp.pad(state_pool, ((0, 0), (0, 0), (POOL_PAD - POOL_HIST, 0), (0, 0)))
    sinks = swa_sinks.astype(F32).reshape(depth, SWA_HEADS, 1, 1)
    sink_p = jnp.broadcast_to(sinks, (depth, SWA_HEADS, CHUNK, 128)).reshape(depth, SWA_HEADS * CHUNK, 128)
    sink_s = jnp.broadcast_to(sinks, (depth, SWA_HEADS, ss, 128)).reshape(depth, SWA_HEADS * ss, 128)
    norm3 = lambda g: g.reshape(depth, 1, D_MODEL)
    ffn_a_g, mix_g, mem_g, ffn_b_g = norm3(ffn_a_norm), norm3(mix_norm), norm3(mem_norm), norm3(ffn_b_norm)
    pool_s3 = pool_scale.reshape(depth, 1, POOL_WIDTH)
    cache_k4 = cache_swa_k.reshape(depth, bs, n_cache, SWA_KV_WIDTH)
    cache_v4 = cache_swa_v.reshape(depth, bs, n_cache, SWA_KV_WIDTH)
    cache_mk3 = cache_mem_k.reshape(depth, bs * N_MEM, MEM_WIDTH)
    cache_mv3 = cache_mem_v.reshape(depth, bs * N_MEM, MEM_WIDTH)

    y_pool = jnp.zeros((t_all, POOL_WIDTH), BF16)
    y_swa = jnp.zeros((t_all, SWA_WIDTH), BF16)
    y_mem = jnp.zeros((t_all, MEM_WIDTH), BF16)

    p_k, p_v, p_pool, p_mk, p_mv, s_k, s_v, s_pool = ([] for _ in range(8))
    fin3 = final_norm.reshape(1, 1, D_MODEL)
    hb, x, sumsq = _join(x_prompt.reshape(tp, D_MODEL), x_sample.reshape(ts, D_MODEL), ffn_a_g, 0)
    stream = (x, hb, sumsq)
    for l in range(depth):
        x, h, sumsq = _ffn_half(stream, ffn_a_w_in, ffn_a_w_out, l, mix_g, l)

        rs = _row_scale(sumsq)
        u = _proj(h, w_in, l, OFF_U, POOL_WIDTH, F32, rs)
        q = _proj(h, w_in, l, OFF_Q, SWA_WIDTH, BF16, rs, scale=SWA_HEAD_DIM ** -0.5)
        kv, kt, vt = _proj_kv(h, w_in, l, expand, rs)
        mq = _proj(h, w_in, l, OFF_MQ, MEM_WIDTH, BF16, rs, scale=MEM_HEAD_DIM ** -0.5)
        gates = _proj_staged(h, w_in, l, OFF_GATE, 3 * D_MODEL, F32, rs)

        y_pool = _pool(u, hist_p, pool_w, pool_s3, l, 0, bp, sp, 1, 0, t_all, prev=y_pool)
        y_pool = _pool(u, hist_s, pool_w, pool_s3, l, POOL_HIST, bs, ss, bs, tp, t_all, prev=y_pool)
        u_s = u[tp:].reshape(bs, ss, POOL_WIDTH)
        p_pool.append(u[:tp].reshape(bp, sp, POOL_WIDTH)[:, sp - POOL_HIST:])
        s_pool.append(jnp.concatenate([state_pool[l], u_s], axis=1)[:, ss:])

        y_swa = _swa_prompt(q, kt, vt, bias_p, sink_p, l, bp, n_chunk, y_swa)
        y_swa = _swa_sample(q, cache_k4, cache_v4, kv, expand, bias_s, sink_s, l, tp, bs, ss, y_swa)
        k_new, v_new = kv[:, :SWA_KV_WIDTH], kv[:, SWA_KV_WIDTH:]
        p_k.append(k_new[:tp].reshape(bp, sp, *kvh)[:, sp - SWA_WINDOW:])
        p_v.append(v_new[:tp].reshape(bp, sp, *kvh)[:, sp - SWA_WINDOW:])
        s_k.append(jnp.concatenate([cache_k4[l][:, ss:], k_new[tp:].reshape(bs, ss, SWA_KV_WIDTH)],
                                   axis=1).reshape(bs, n_cache, *kvh))
        s_v.append(jnp.concatenate([cache_v4[l][:, ss:], v_new[tp:].reshape(bs, ss, SWA_KV_WIDTH)],
                                   axis=1).reshape(bs, n_cache, *kvh))

        mkv = _proj(_rmsnorm(mem_x, (mem_g, l), BF16, bm=256), w_mem_kv, l, 0, 2 * MEM_WIDTH, F32, bm=256)
        y_mem = _mem_attend(mq, pl.BlockSpec((N_MEM, MEM_WIDTH), lambda b, s: (b, 0)),
                            pl.BlockSpec((N_MEM, MEM_WIDTH), lambda b, s: (b, 1)),
                            mkv, mkv, 0, bp, sp, 512, prev=y_mem)
        cache_spec = pl.BlockSpec((None, N_MEM, MEM_WIDTH), lambda b, s: (l, b, 0))
        y_mem = _mem_attend(mq, cache_spec, cache_spec, cache_mk3, cache_mv3, tp, bs, ss, ss,
                            prev=y_mem)
        p_mk.append(mkv[:, :MEM_WIDTH].reshape(bp, N_MEM, MEM_HEADS, MEM_HEAD_DIM))
        p_mv.append(mkv[:, MEM_WIDTH:].reshape(bp, N_MEM, MEM_HEADS, MEM_HEAD_DIM))

        m = _merge(gates, y_pool, y_swa, y_mem, w_branch_pool, w_branch_swa, w_branch_mem, l)
        stream = _proj_resid(m, w_out, l, x, ffn_b_g, l)

        g_next = (ffn_a_g, l + 1) if l + 1 < depth else (fin3, 0)
        stream = _ffn_half(stream, ffn_b_w_in, ffn_b_w_out, l, *g_next)

    x = stream[0]
    fin = (final_norm.reshape(1, D_MODEL), None)
    y_p = _rmsnorm(x, fin, F32, bm=256, row0=0, rows=tp)
    y_s = _rmsnorm(x, fin, F32, bm=256, row0=tp, rows=ts)
    return (y_p.reshape(bp, sp, D_MODEL), y_s.reshape(bs, ss, D_MODEL),
            jnp.stack(p_k), jnp.stack(p_v), jnp.stack(p_pool), jnp.stack(p_mk), jnp.stack(p_mv),
            jnp.stack(s_k), jnp.stack(s_v), jnp.stack(s_pool))
```

```python
import functools
import math

import jax
import jax.numpy as jnp
from jax import lax
from jax.experimental import pallas as pl
from jax.experimental.pallas import tpu as pltpu

F32 = jnp.float32
BF16 = jnp.bfloat16

D_MODEL = 4096
D_FF = 11008
CHUNK = 64
EPS = 1e-6
POOL_WINDOWS = (2, 4, 8, 16)
POOL_GROUP = 256
POOL_WIDTH = 1024
POOL_HIST = 15
POOL_PAD = 16
SWA_WINDOW = 128
SWA_HEAD_DIM = 64
SWA_HEADS = 32
SWA_KV_HEADS = 4
SWA_GROUP = 8
SWA_WIDTH = 2048
SWA_KV_WIDTH = 256
SWA_SPAN = SWA_GROUP * SWA_HEAD_DIM
SWA_BAND = 256
N_MEM = 256
MEM_HEADS = 4
MEM_WIDTH = 1024
MEM_HEAD_DIM = 256
REL_BUCKETS = 32
REL_MAX_DIST = 128
NEG = -1e30

OFF_U = 0
OFF_Q = OFF_U + POOL_WIDTH
OFF_KV = OFF_Q + SWA_WIDTH
OFF_MQ = OFF_KV + 2 * SWA_KV_WIDTH
OFF_GATE = OFF_MQ + MEM_WIDTH

VMEM_LIMIT_BYTES = 58 * 1024 * 1024
VMEM_LIMIT_STAGED_BYTES = 62 * 1024 * 1024
BM = 1088
WB = 512
NW = 1
BM_MERGE = 1088
FB = 256
BM_DOWN = 544
BN_DOWN = 512
LANES = 128
ROW_CHUNK = 272
CAST_ROWS = 512


def _cparams(n_axes, vmem_limit_bytes=VMEM_LIMIT_BYTES):
    return pltpu.CompilerParams(
        dimension_semantics=("arbitrary",) * n_axes,
        vmem_limit_bytes=vmem_limit_bytes,
    )


def _any_spec():
    return pl.BlockSpec(memory_space=pl.ANY)


def _rms_kernel(x_ref, g_ref, o_ref):
    x = x_ref[...]
    ms = jnp.mean(x * x, axis=-1, keepdims=True)
    o_ref[...] = (x * lax.rsqrt(ms + EPS) * g_ref[...]).astype(o_ref.dtype)


def _rmsnorm(x, g, out_dtype, bm=544, row0=0, rows=None):
    d = x.shape[1]
    rows = x.shape[0] if rows is None else rows
    blk0 = row0 // bm
    g_arr, l = g
    if l is None:
        gspec = pl.BlockSpec((1, d), lambda i: (0, 0))
    else:
        gspec = pl.BlockSpec((None, 1, d), lambda i: (l, 0, 0))
    return pl.pallas_call(
        _rms_kernel,
        out_shape=jax.ShapeDtypeStruct((rows, d), out_dtype),
        grid=(rows // bm,),
        in_specs=[pl.BlockSpec((bm, d), lambda i: (blk0 + i, 0)), gspec],
        out_specs=pl.BlockSpec((bm, d), lambda i: (i, 0)),
        compiler_params=_cparams(1),
        name="rmsnorm",
    )(x, g_arr)


def _lane_partial_sumsq(x):
    sq = x * x
    part = sq[:, 0:LANES]
    for c0 in range(LANES, x.shape[1], LANES):
        part = part + sq[:, c0:c0 + LANES]
    return part


def _join_kernel(xp_ref, xs_ref, g_ref, hb_ref, x_ref, ss_ref, *, n_first):
    def emit(src_ref):
        x = src_ref[...]
        hb_ref[...] = (x * g_ref[...]).astype(hb_ref.dtype)
        x_ref[...] = x
        ss_ref[...] = _lane_partial_sumsq(x)

    i = pl.program_id(0)

    @pl.when(i < n_first)
    def _():
        emit(xp_ref)

    @pl.when(i >= n_first)
    def _():
        emit(xs_ref)


def _join(xp, xs, g, l, bm=256):
    d = xp.shape[1]
    n_first = xp.shape[0] // bm
    n_second = xs.shape[0] // bm
    rows = xp.shape[0] + xs.shape[0]
    out = pl.BlockSpec((bm, d), lambda i: (i, 0))
    return pl.pallas_call(
        functools.partial(_join_kernel, n_first=n_first),
        out_shape=(jax.ShapeDtypeStruct((rows, d), BF16), jax.ShapeDtypeStruct((rows, d), F32),
                   jax.ShapeDtypeStruct((1, rows, LANES), F32)),
        grid=(n_first + n_second,),
        in_specs=[pl.BlockSpec((bm, d), lambda i: (jnp.minimum(i, n_first - 1), 0)),
                  pl.BlockSpec((bm, d), lambda i: (jnp.maximum(i - n_first, 0), 0)),
                  pl.BlockSpec((None, 1, d), lambda i: (l, 0, 0))],
        out_specs=(out, out, pl.BlockSpec((None, bm, LANES), lambda i: (0, i, 0))),
        compiler_params=_cparams(1),
        name="join",
    )(xp, xs, g)


def _row_scale_kernel(ss_ref, r_ref):
    tot = jnp.sum(ss_ref[...], axis=0)
    ms = jnp.sum(tot, axis=1, keepdims=True) * (1.0 / D_MODEL)
    r_ref[...] = jnp.broadcast_to(lax.rsqrt(ms + EPS), r_ref.shape)


def _row_scale(ss, bm=BM):
    p, rows, _ = ss.shape
    return pl.pallas_call(
        _row_scale_kernel,
        out_shape=jax.ShapeDtypeStruct((rows, LANES), F32),
        grid=(rows // bm,),
        in_specs=[pl.BlockSpec((p, bm, LANES), lambda i: (0, i, 0))],
        out_specs=pl.BlockSpec((bm, LANES), lambda i: (i, 0)),
        compiler_params=_cparams(1),
        name="row_scale",
    )(ss)


def _stage_weights(w_refs, wb_ref):
    @pl.when(pl.program_id(1) == 0)
    def _():
        wd = w_refs[0].shape[1]
        for t, w in enumerate(w_refs):
            wb_ref[:, t * wd:(t + 1) * wd] = w[...].astype(BF16)


def _wspecs(k, wb, l, first, nw):
    return [pl.BlockSpec((None, k, wb), functools.partial(lambda j, i, t: (l, 0, first + nw * j + t), t=t))
            for t in range(nw)]


def _rspec(bm):
    return pl.BlockSpec((bm, LANES), lambda j, i: (i, 0))


def _row_chunks(bm):
    rc = ROW_CHUNK if bm % ROW_CHUNK == 0 else bm
    return [slice(r0, r0 + rc) for r0 in range(0, bm, rc)]


def _proj_kernel(*refs, nw, scale, scaled_rows):
    a_ref, w_refs = refs[0], refs[1:1 + nw]
    o_ref, wb_ref = refs[-2:]
    _stage_weights(w_refs, wb_ref)
    for rows in _row_chunks(a_ref.shape[0]):
        acc = jnp.dot(a_ref[rows, :], wb_ref[...], preferred_element_type=F32)
        if scaled_rows:
            acc = acc * refs[1 + nw][rows, 0:1]
        if scale != 1.0:
            acc = acc * scale
        o_ref[rows, :] = acc.astype(o_ref.dtype)


def _proj(a, w, l, col0, ncols, out_dtype, rscale=None, scale=1.0, bm=BM, wb=WB, nw=NW):
    m, k = a.shape
    bn = wb * nw
    extra_specs, extra = ([], []) if rscale is None else ([_rspec(bm)], [rscale])
    return pl.pallas_call(
        functools.partial(_proj_kernel, nw=nw, scale=scale, scaled_rows=rscale is not None),
        out_shape=jax.ShapeDtypeStruct((m, ncols), out_dtype),
        grid=(ncols // bn, m // bm),
        in_specs=[pl.BlockSpec((bm, k), lambda j, i: (i, 0))] + _wspecs(k, wb, l, col0 // wb, nw)
        + extra_specs,
        out_specs=pl.BlockSpec((bm, bn), lambda j, i: (i, j)),
        scratch_shapes=[pltpu.VMEM((k, bn), BF16)],
        compiler_params=_cparams(2),
        name="proj",
    )(a, *([w] * nw), *extra)


def _proj_kv_kernel(a_ref, w_ref, e_ref, r_ref, kv_ref, kt_ref, vt_ref, wb_ref):
    _stage_weights((w_ref,), wb_ref)
    acc = jnp.dot(a_ref[...], wb_ref[...], preferred_element_type=F32) * r_ref[:, 0:1]
    kv_ref[...] = acc
    kv = acc.astype(BF16)
    kt_ref[...] = jnp.dot(kv[:, :SWA_KV_WIDTH], e_ref[...], preferred_element_type=F32).astype(BF16)
    vt_ref[...] = jnp.dot(kv[:, SWA_KV_WIDTH:], e_ref[...], preferred_element_type=F32).astype(BF16)


def _proj_kv(a, w, l, expand, rscale, bm=BM // 2):
    m, k = a.shape
    n = 2 * SWA_KV_WIDTH
    row = lambda width: pl.BlockSpec((bm, width), lambda j, i: (i, 0))
    return pl.pallas_call(
        _proj_kv_kernel,
        out_shape=(jax.ShapeDtypeStruct((m, n), F32),
                   jax.ShapeDtypeStruct((m, SWA_WIDTH), BF16),
                   jax.ShapeDtypeStruct((m, SWA_WIDTH), BF16)),
        grid=(1, m // bm),
        in_specs=[row(k)] + _wspecs(k, n, l, OFF_KV // n, 1)
        + [pl.BlockSpec((SWA_KV_WIDTH, SWA_WIDTH), lambda j, i: (0, 0)), _rspec(bm)],
        out_specs=(row(n), row(SWA_WIDTH), row(SWA_WIDTH)),
        scratch_shapes=[pltpu.VMEM((k, n), BF16)],
        compiler_params=_cparams(2),
        name="proj_kv",
    )(a, w, expand, rscale)


def _emit_stream(rows, x_new, g_ref, o_ref, hb_ref):
    o_ref[rows, :] = x_new
    hb_ref[rows, :] = (x_new * g_ref[...]).astype(hb_ref.dtype)


def _proj_resid_kernel(*refs, nw):
    a_ref, w_refs = refs[0], refs[1:1 + nw]
    r_ref, g_ref, o_ref, hb_ref, ss_ref, wb_ref = refs[1 + nw:]
    _stage_weights(w_refs, wb_ref)
    for rows in _row_chunks(a_ref.shape[0]):
        x_new = r_ref[rows, :] + jnp.dot(a_ref[rows, :], wb_ref[...], preferred_element_type=F32)
        _emit_stream(rows, x_new, g_ref, o_ref, hb_ref)
        ss_ref[rows, :] = _lane_partial_sumsq(x_new)


def _proj_resid(a, w, l, resid, g, gl, bm=BM, wb=WB, nw=NW):
    m, k = a.shape
    n = w.shape[2]
    bn = wb * nw
    nj = n // bn
    tile = pl.BlockSpec((bm, bn), lambda j, i: (i, j))
    return pl.pallas_call(
        functools.partial(_proj_resid_kernel, nw=nw),
        out_shape=(jax.ShapeDtypeStruct((m, n), F32), jax.ShapeDtypeStruct((m, n), BF16),
                   jax.ShapeDtypeStruct((nj, m, LANES), F32)),
        grid=(nj, m // bm),
        in_specs=[pl.BlockSpec((bm, k), lambda j, i: (i, 0))] + _wspecs(k, wb, l, 0, nw)
        + [tile, pl.BlockSpec((None, 1, bn), lambda j, i: (gl, 0, j))],
        out_specs=(tile, tile, pl.BlockSpec((None, bm, LANES), lambda j, i: (j, i, 0))),
        scratch_shapes=[pltpu.VMEM((k, bn), BF16)],
        compiler_params=_cparams(2),
        name="proj_resid",
    )(a, *([w] * nw), resid, g)


def _staged_weights(i, j, nj, start_copies, wait_copies, stage_ref, wb_ref):
    @pl.when(i == 0)
    def _():
        @pl.when(j == 0)
        def _():
            start_copies(0)

        wait_copies(j)
        for r0 in range(0, wb_ref.shape[0], CAST_ROWS):
            wb_ref[r0:r0 + CAST_ROWS, :] = stage_ref[r0:r0 + CAST_ROWS, :].astype(BF16)

        @pl.when(j + 1 < nj)
        def _():
            start_copies(j + 1)


def _proj_staged_kernel(a_ref, w_hbm, r_ref, o_ref, stage_ref, wb_ref, sem, *, l, nj, col0, scale):
    j = pl.program_id(0)
    i = pl.program_id(1)
    bn = o_ref.shape[1]

    def copy(jj):
        col = pl.multiple_of(col0 + jj * bn, WB)
        return pltpu.make_async_copy(w_hbm.at[l, :, pl.ds(col, bn)], stage_ref, sem.at[0])

    _staged_weights(i, j, nj, lambda jj: copy(jj).start(), lambda jj: copy(jj).wait(),
                    stage_ref, wb_ref)
    rc = a_ref.shape[0] // 4
    for r0 in range(0, a_ref.shape[0], rc):
        acc = jnp.dot(a_ref[r0:r0 + rc, :], wb_ref[...], preferred_element_type=F32)
        acc = acc * r_ref[r0:r0 + rc, 0:1]
        if scale != 1.0:
            acc = acc * scale
        o_ref[r0:r0 + rc, :] = acc.astype(o_ref.dtype)


def _proj_staged(a, w, l, col0, ncols, out_dtype, rscale, scale=1.0, bm=BM, bn=2 * WB):
    m, k = a.shape
    nj = ncols // bn
    assert col0 % WB == 0 and ncols % bn == 0
    return pl.pallas_call(
        functools.partial(_proj_staged_kernel, l=l, nj=nj, col0=col0, scale=scale),
        out_shape=jax.ShapeDtypeStruct((m, ncols), out_dtype),
        grid=(nj, m // bm),
        in_specs=[pl.BlockSpec((bm, k), lambda j, i: (i, 0)), _any_spec(), _rspec(bm)],
        out_specs=pl.BlockSpec((bm, bn), lambda j, i: (i, j)),
        scratch_shapes=[pltpu.VMEM((k, bn), F32), pltpu.VMEM((k, bn), BF16),
                        pltpu.SemaphoreType.DMA((1,))],
        compiler_params=_cparams(2, VMEM_LIMIT_STAGED_BYTES),
        name="proj_staged",
    )(a, w, rscale)


def _swiglu_kernel(a_ref, w_hbm, wo_ref, r_ref, o_ref, wob_ref, stage_ref, wb_ref, sem,
                   *, l, nj, fn, tail):
    j = pl.program_id(0)
    i = pl.program_id(1)

    def copies(jj, width):
        col = pl.multiple_of(jj * fn, fn)
        return (pltpu.make_async_copy(w_hbm.at[l, :, pl.ds(col, width)],
                                      stage_ref.at[:, :width], sem.at[0]),
                pltpu.make_async_copy(w_hbm.at[l, :, pl.ds(D_FF + col, width)],
                                      stage_ref.at[:, fn:fn + width], sem.at[1]))

    def each(jj, method):
        @pl.when(jj < nj - 1)
        def _():
            for c in copies(jj, fn):
                getattr(c, method)()

        @pl.when(jj == nj - 1)
        def _():
            for c in copies(jj, tail):
                getattr(c, method)()

    _staged_weights(i, j, nj, lambda jj: each(jj, "start"), lambda jj: each(jj, "wait"),
                    stage_ref, wb_ref)
    wob_ref[...] = wo_ref[...].astype(BF16)

    def swiglu(width):
        rc = a_ref.shape[0] // 4
        for r0 in range(0, a_ref.shape[0], rc):
            a = a_ref[r0:r0 + rc, :]
            rs = r_ref[r0:r0 + rc, 0:1]
            if width == fn:
                acc = jnp.dot(a, wb_ref[...], preferred_element_type=F32) * rs
                gate, up = acc[:, :fn], acc[:, fn:]
            else:
                gate = jnp.dot(a, wb_ref[:, :width], preferred_element_type=F32) * rs
                up = jnp.dot(a, wb_ref[:, fn:fn + width], preferred_element_type=F32) * rs
                o_ref[r0:r0 + rc, width:] = jnp.zeros((rc, fn - width), o_ref.dtype)
            o_ref[r0:r0 + rc, :width] = (gate * jax.nn.sigmoid(gate) * up).astype(o_ref.dtype)

    @pl.when(j < nj - 1)
    def _():
        swiglu(fn)

    @pl.when(j == nj - 1)
    def _():
        swiglu(tail)


def _ffn_up(a, w_in, w_out, l, rscale, bm=BM, fn=2 * FB):
    m, k = a.shape
    nj = pl.cdiv(D_FF, fn)
    tail = D_FF - (nj - 1) * fn
    ni = m // bm
    n_out = w_out.shape[2]
    oc = n_out // ni
    return pl.pallas_call(
        functools.partial(_swiglu_kernel, l=l, nj=nj, fn=fn, tail=tail),
        out_shape=(jax.ShapeDtypeStruct((m, nj * fn), BF16),
                   jax.ShapeDtypeStruct((D_FF, n_out), BF16)),
        grid=(nj, ni),
        in_specs=[pl.BlockSpec((bm, k), lambda j, i: (i, 0)),
                  _any_spec(),
                  pl.BlockSpec((None, fn, oc), lambda j, i: (l, j, i)),
                  _rspec(bm)],
        out_specs=(pl.BlockSpec((bm, fn), lambda j, i: (i, j)),
                   pl.BlockSpec((fn, oc), lambda j, i: (j, i))),
        scratch_shapes=[pltpu.VMEM((k, 2 * fn), F32), pltpu.VMEM((k, 2 * fn), BF16),
                        pltpu.SemaphoreType.DMA((2,))],
        compiler_params=_cparams(2, VMEM_LIMIT_STAGED_BYTES),
        name="ffn_up",
    )(a, w_in, w_out, rscale)


def _ffn_down_kernel(a_ref, w_ref, r_ref, g_ref, o_ref, hb_ref, ss_ref):
    @pl.when(pl.program_id(1) == 0)
    def _():
        ss_ref[...] = jnp.zeros_like(ss_ref)

    for rows in _row_chunks(a_ref.shape[0]):
        x_new = r_ref[rows, :] + 0.5 * jnp.dot(a_ref[rows, :], w_ref[...], preferred_element_type=F32)
        _emit_stream(rows, x_new, g_ref, o_ref, hb_ref)
        ss_ref[rows, :] += _lane_partial_sumsq(x_new)


def _ffn_down(a, w_bf16, resid, g, gl, bm=BM_DOWN, bn=BN_DOWN):
    m = a.shape[0]
    k, n = w_bf16.shape
    tile = pl.BlockSpec((bm, bn), lambda i, j: (i, j))
    return pl.pallas_call(
        _ffn_down_kernel,
        out_shape=(jax.ShapeDtypeStruct((m, n), F32), jax.ShapeDtypeStruct((m, n), BF16),
                   jax.ShapeDtypeStruct((1, m, LANES), F32)),
        grid=(m // bm, n // bn),
        in_specs=[pl.BlockSpec((bm, k), lambda i, j: (i, 0)),
                  pl.BlockSpec((k, bn), lambda i, j: (0, j)), tile,
                  pl.BlockSpec((None, 1, bn), lambda i, j: (gl, 0, j))],
        out_specs=(tile, tile, pl.BlockSpec((None, bm, LANES), lambda i, j: (0, i, 0))),
        compiler_params=_cparams(2),
        name="ffn_down",
    )(a, w_bf16, resid, g)


def _merge_kernel(yp_ref, ya_ref, ym_ref, wp_ref, wa_ref, wm_ref, gp_ref, ga_ref, gm_ref,
                  o_ref, wpb_ref, wab_ref, wmb_ref):
    @pl.when(pl.program_id(1) == 0)
    def _():
        wpb_ref[...] = wp_ref[...].astype(BF16)
        wab_ref[...] = wa_ref[...].astype(BF16)
        wmb_ref[...] = wm_ref[...].astype(BF16)

    for rows in _row_chunks(yp_ref.shape[0]):
        m = jax.nn.sigmoid(gp_ref[rows, :]) * jnp.dot(yp_ref[rows, :], wpb_ref[...],
                                                      preferred_element_type=F32)
        m = m + jax.nn.sigmoid(ga_ref[rows, :]) * jnp.dot(ya_ref[rows, :], wab_ref[...],
                                                          preferred_element_type=F32)
        m = m + jax.nn.sigmoid(gm_ref[rows, :]) * jnp.dot(ym_ref[rows, :], wmb_ref[...],
                                                          preferred_element_type=F32)
        o_ref[rows, :] = m.astype(o_ref.dtype)


def _merge(gates, y_pool, y_swa, y_mem, w_bp, w_bs, w_bm, l, bm=BM_MERGE, bn=WB):
    m = gates.shape[0]
    nb = D_MODEL // bn
    act = lambda k: pl.BlockSpec((bm, k), lambda j, i: (i, 0))
    wgt = lambda k: pl.BlockSpec((None, k, bn), lambda j, i: (l, 0, j))
    gate = lambda t: pl.BlockSpec((bm, bn), lambda j, i: (i, j + t * nb))
    return pl.pallas_call(
        _merge_kernel,
        out_shape=jax.ShapeDtypeStruct((m, D_MODEL), BF16),
        grid=(nb, m // bm),
        in_specs=[act(POOL_WIDTH), act(SWA_WIDTH), act(MEM_WIDTH),
                  wgt(POOL_WIDTH), wgt(SWA_WIDTH), wgt(MEM_WIDTH),
                  gate(0), gate(1), gate(2)],
        out_specs=pl.BlockSpec((bm, bn), lambda j, i: (i, j)),
        scratch_shapes=[pltpu.VMEM((POOL_WIDTH, bn), BF16), pltpu.VMEM((SWA_WIDTH, bn), BF16),
                        pltpu.VMEM((MEM_WIDTH, bn), BF16)],
        compiler_params=_cparams(2),
        name="merge",
    )(y_pool, y_swa, y_mem, w_bp, w_bs, w_bm, gates, gates, gates)


def _pool_kernel(*refs, bb, s_len, n_hist, rc):
    u_ref, h_ref, w_ref, s_ref = refs[:4]
    o_ref, full_ref = refs[-2:]
    full_ref[:, 0:POOL_PAD, :] = h_ref[...]
    full_ref[:, POOL_PAD:, :] = u_ref[...].reshape(bb, s_len, POOL_WIDTH)
    for g, win in enumerate(POOL_WINDOWS):
        lanes = slice(g * POOL_GROUP, (g + 1) * POOL_GROUP)
        wg = w_ref[g].astype(BF16)
        sc = s_ref[:, lanes]
        for r0 in range(0, s_len, rc):
            base = POOL_PAD + r0
            cur = full_ref[:, base:base + rc, lanes]
            acc = cur
            for d in range(1, win):
                acc = acc + full_ref[:, base - d:base - d + rc, lanes]
            t = lax.broadcasted_iota(jnp.int32, (bb, rc, POOL_GROUP), 1) + (r0 + n_hist + 1)
            cnt = jnp.minimum(t, win).astype(F32)
            mixed = (acc / cnt - cur).reshape(bb * rc, POOL_GROUP).astype(BF16)
            y = jnp.dot(mixed, wg, preferred_element_type=F32) * sc
            o_ref[r0:r0 + bb * rc, lanes] = y.astype(o_ref.dtype)


def _pool(u, hist, pool_w, pool_scale, l, n_hist, n_batch, s_len, bb, row0, out_rows, prev=None):
    rows = bb * s_len
    blk0 = row0 // rows
    rc = min(s_len, 256)
    assert bb == 1 or rc == s_len
    if hist.ndim == 4:
        hspec = pl.BlockSpec((None, bb, POOL_PAD, POOL_WIDTH), lambda b: (l, b, 0, 0))
    else:
        hspec = pl.BlockSpec((bb, POOL_PAD, POOL_WIDTH), lambda b: (b, 0, 0))
    in_specs = [pl.BlockSpec((rows, POOL_WIDTH), lambda b: (blk0 + b, 0)), hspec,
                pl.BlockSpec((None, len(POOL_WINDOWS), POOL_GROUP, POOL_GROUP), lambda b: (l, 0, 0, 0)),
                pl.BlockSpec((None, 1, POOL_WIDTH), lambda b: (l, 0, 0))]
    args = [u, hist, pool_w, pool_scale]
    aliases = {}
    if prev is not None:
        in_specs.append(_any_spec())
        args.append(prev)
        aliases = {4: 0}
    return pl.pallas_call(
        functools.partial(_pool_kernel, bb=bb, s_len=s_len, n_hist=n_hist, rc=rc),
        out_shape=jax.ShapeDtypeStruct((out_rows, POOL_WIDTH), BF16),
        grid=(n_batch // bb,),
        in_specs=in_specs,
        out_specs=pl.BlockSpec((rows, POOL_WIDTH), lambda b: (blk0 + b, 0)),
        scratch_shapes=[pltpu.VMEM((bb, POOL_PAD + s_len, POOL_WIDTH), F32)],
        input_output_aliases=aliases,
        compiler_params=_cparams(1),
        name="pool",
    )(*args)


def _rel_bucket(rel):
    half = REL_BUCKETS // 2
    max_exact = half // 2
    n = jnp.abs(rel)
    nf = jnp.maximum(n, 1).astype(jnp.float32)
    large = max_exact + (jnp.log(nf / max_exact) / math.log(REL_MAX_DIST / max_exact)
                         * (half - max_exact)).astype(jnp.int32)
    large = jnp.minimum(large, half - 1)
    return jnp.where(rel > 0, half, 0) + jnp.where(n < max_exact, n, large)


def _bias_kernel(tab_ref, idx_ref, o_ref):
    h = pl.program_id(0)
    idx = idx_ref[...]
    acc = jnp.where(idx < 0, NEG, 0.0).astype(F32)
    for b in range(REL_BUCKETS):
        acc = acc + jnp.where(idx == b, tab_ref[b, h], 0.0)
    o_ref[0] = acc


def _bias_table(table, sq, n_keys, key_zero):
    band_lo = key_zero - SWA_WINDOW
    j = jnp.arange(SWA_BAND)[None, :]
    qi = jnp.arange(sq)[:, None]
    idx = _rel_bucket(j - key_zero - qi)
    idx = jnp.where((j >= band_lo) & (j < band_lo + n_keys), idx, -1).astype(jnp.int32)
    return pl.pallas_call(
        _bias_kernel,
        out_shape=jax.ShapeDtypeStruct((SWA_HEADS, sq, SWA_BAND), F32),
        grid=(SWA_HEADS,),
        in_specs=[pl.BlockSpec(memory_space=pltpu.SMEM),
                  pl.BlockSpec((sq, SWA_BAND), lambda h: (0, 0))],
        out_specs=pl.BlockSpec((1, sq, SWA_BAND), lambda h: (h, 0, 0)),
        compiler_params=_cparams(1),
        name="rel_bias",
    )(table, idx)


def _swa_core(q, kband, vband, b_ref, s_ref, lo, sq):
    rows = SWA_GROUP * sq
    lane_grp = lax.shift_right_logical(lax.broadcasted_iota(jnp.int32, (sq, SWA_SPAN), 1), 6)
    col_ok = None
    if lo is not None:
        col_ok = lax.broadcasted_iota(jnp.int32, (SWA_HEADS * sq, SWA_BAND), 1) >= lo
    parts = []
    for h in range(SWA_KV_HEADS):
        lanes = slice(h * SWA_SPAN, (h + 1) * SWA_SPAN)
        qs = q[:, lanes]
        qstack = jnp.concatenate(
            [jnp.where(lane_grp == g, qs, jnp.zeros_like(qs)) for g in range(SWA_GROUP)], axis=0)
        parts.append(lax.dot_general(qstack, kband[:, lanes], (((1,), (1,)), ((), ())),
                                     preferred_element_type=F32))
    s = jnp.concatenate(parts, axis=0) + b_ref[...].reshape(SWA_HEADS * sq, SWA_BAND)
    if col_ok is not None:
        s = jnp.where(col_ok, s, NEG)
    sk = s_ref[:, 0:1]
    m = jnp.maximum(jnp.max(s, axis=1, keepdims=True), sk)
    e = jnp.exp(s - m)
    z = jnp.sum(e, axis=1, keepdims=True) + jnp.exp(sk - m)
    p = (e / z).astype(BF16)
    outs = []
    for h in range(SWA_KV_HEADS):
        lanes = slice(h * SWA_SPAN, (h + 1) * SWA_SPAN)
        r = jnp.dot(p[h * rows:(h + 1) * rows], vband[:, lanes], preferred_element_type=F32)
        o = jnp.where(lane_grp == 0, r[0:sq], 0.0)
        for g in range(1, SWA_GROUP):
            o = o + jnp.where(lane_grp == g, r[g * sq:(g + 1) * sq], 0.0)
        outs.append(o)
    return jnp.concatenate(outs, axis=1)


def _swa_prompt_kernel(*refs, n_kb):
    q_ref = refs[0]
    k_refs = refs[1:1 + n_kb]
    v_refs = refs[1 + n_kb:1 + 2 * n_kb]
    b_ref, s_ref = refs[1 + 2 * n_kb:3 + 2 * n_kb]
    o_ref = refs[-1]
    sq = q_ref.shape[0]
    kband = jnp.concatenate([r[...] for r in k_refs], axis=0)
    vband = jnp.concatenate([r[...] for r in v_refs], axis=0)
    lo = jnp.maximum(0, (n_kb - 1 - pl.program_id(1)) * CHUNK)
    o = _swa_core(q_ref[...], kband, vband, b_ref, s_ref, lo, sq)
    o_ref[...] = o.astype(o_ref.dtype)


def _swa_prompt(q, kt, vt, bias, sink_cols, l, n_batch, n_chunk, prev):
    sq = CHUNK
    n_kb = SWA_BAND // CHUNK
    qspec = pl.BlockSpec((sq, SWA_WIDTH), lambda b, c: (b * n_chunk + c, 0))

    def band(d):
        return pl.BlockSpec((CHUNK, SWA_WIDTH),
                            lambda b, c: (b * n_chunk + jnp.maximum(c - (n_kb - 1) + d, 0), 0))

    return pl.pallas_call(
        functools.partial(_swa_prompt_kernel, n_kb=n_kb),
        out_shape=jax.ShapeDtypeStruct(prev.shape, BF16),
        grid=(n_batch, n_chunk),
        in_specs=[qspec] + [band(d) for d in range(n_kb)] * 2
        + [pl.BlockSpec((SWA_HEADS, sq, SWA_BAND), lambda b, c: (0, 0, 0)),
           pl.BlockSpec((None, SWA_HEADS * sq, 128), lambda b, c: (l, 0, 0)),
           _any_spec()],
        out_specs=qspec,
        input_output_aliases={3 + 2 * n_kb: 0},
        compiler_params=_cparams(2),
        name="swa_prompt",
    )(q, *([kt] * n_kb), *([vt] * n_kb), bias, sink_cols, prev)


def _swa_sample_kernel(q_ref, ck_ref, cv_ref, kvn_ref, e_ref, b_ref, s_ref, prev_ref, o_ref):
    del prev_ref
    sq = q_ref.shape[0]
    n_cache = ck_ref.shape[0]
    kvn = kvn_ref[...]
    zpad = jnp.zeros((SWA_BAND - n_cache - sq, SWA_KV_WIDTH), F32)
    kk = jnp.concatenate([ck_ref[...], kvn[:, :SWA_KV_WIDTH], zpad], axis=0).astype(BF16)
    vv = jnp.concatenate([cv_ref[...], kvn[:, SWA_KV_WIDTH:], zpad], axis=0).astype(BF16)
    kband = jnp.dot(kk, e_ref[...], preferred_element_type=F32).astype(BF16)
    vband = jnp.dot(vv, e_ref[...], preferred_element_type=F32).astype(BF16)
    o = _swa_core(q_ref[...], kband, vband, b_ref, s_ref, None, sq)
    o_ref[...] = o.astype(o_ref.dtype)


def _swa_sample(q, cache_k, cache_v, kv, expand, bias, sink_cols, l, row0, n_batch, sq, prev):
    blk0 = row0 // sq
    n_cache = cache_k.shape[2]
    cspec = pl.BlockSpec((None, None, n_cache, SWA_KV_WIDTH), lambda b: (l, b, 0, 0))
    return pl.pallas_call(
        _swa_sample_kernel,
        out_shape=jax.ShapeDtypeStruct(prev.shape, BF16),
        grid=(n_batch,),
        in_specs=[pl.BlockSpec((sq, SWA_WIDTH), lambda b: (blk0 + b, 0)), cspec, cspec,
                  pl.BlockSpec((sq, 2 * SWA_KV_WIDTH), lambda b: (blk0 + b, 0)),
                  pl.BlockSpec((SWA_KV_WIDTH, SWA_WIDTH), lambda b: (0, 0)),
                  pl.BlockSpec((SWA_HEADS, sq, SWA_BAND), lambda b: (0, 0, 0)),
                  pl.BlockSpec((None, SWA_HEADS * sq, 128), lambda b: (l, 0, 0)),
                  _any_spec()],
        out_specs=pl.BlockSpec((sq, SWA_WIDTH), lambda b: (blk0 + b, 0)),
        input_output_aliases={7: 0},
        compiler_params=_cparams(1),
        name="swa_sample",
    )(q, cache_k, cache_v, kv, expand, bias, sink_cols, prev)


def _mem_kernel(*refs):
    q_ref, k_ref, v_ref = refs[:3]
    o_ref = refs[-1]
    for h in range(MEM_HEADS):
        lanes = slice(h * MEM_HEAD_DIM, (h + 1) * MEM_HEAD_DIM)
        kh = k_ref[:, lanes].astype(BF16)
        vh = v_ref[:, lanes].astype(BF16)
        s = lax.dot_general(q_ref[:, lanes], kh, (((1,), (1,)), ((), ())),
                            preferred_element_type=F32)
        m = jnp.max(s, axis=1, keepdims=True)
        e = jnp.exp(s - m)
        p = (e / jnp.sum(e, axis=1, keepdims=True)).astype(BF16)
        o_ref[:, lanes] = jnp.dot(p, vh, preferred_element_type=F32).astype(o_ref.dtype)


def _mem_attend(mq, kspec, vspec, k, v, row0, n_batch, s_len, bq, prev=None):
    per = s_len // bq
    blk0 = row0 // bq
    qspec = pl.BlockSpec((bq, MEM_WIDTH), lambda b, s: (blk0 + b * per + s, 0))
    in_specs = [qspec, kspec, vspec]
    args = [mq, k, v]
    aliases = {}
    if prev is not None:
        in_specs.append(_any_spec())
        args.append(prev)
        aliases = {3: 0}
    return pl.pallas_call(
        _mem_kernel,
        out_shape=jax.ShapeDtypeStruct(mq.shape, BF16),
        grid=(n_batch, per),
        in_specs=in_specs,
        out_specs=qspec,
        input_output_aliases=aliases,
        compiler_params=_cparams(2),
        name="mem_attend",
    )(*args)


def _head_expansion():
    src = jnp.arange(SWA_KV_WIDTH)
    dst = jnp.arange(SWA_WIDTH)
    dst_src = (dst // SWA_SPAN) * SWA_HEAD_DIM + dst % SWA_HEAD_DIM
    return (src[:, None] == dst_src[None, :]).astype(BF16)


def _ffn_half(stream, w_in, w_out, l, g_next, gl_next):
    x, hb, ss = stream
    hh, w_out_bf16 = _ffn_up(hb, w_in, w_out, l, _row_scale(ss))
    return _ffn_down(hh, w_out_bf16, x, g_next, gl_next)


def kernel(x_prompt, x_sample, cache_swa_k, cache_swa_v, state_pool, cache_mem_k, cache_mem_v,
           mem_prompt, ffn_a_norm, ffn_a_w_in, ffn_a_w_out, mix_norm, mem_norm, w_in, pool_w,
           pool_scale, swa_sinks, rel_bias, w_mem_kv, w_branch_pool, w_branch_swa, w_branch_mem,
           w_out, ffn_b_norm, ffn_b_w_in, ffn_b_w_out, final_norm):
    bp, sp, _ = x_prompt.shape
    bs, ss, _ = x_sample.shape
    depth = w_in.shape[0]
    tp = bp * sp
    ts = bs * ss
    t_all = tp + ts
    n_chunk = sp // CHUNK
    n_cache = cache_swa_k.shape[2]
    kvh = (SWA_KV_HEADS, SWA_HEAD_DIM)

    mem_x = mem_prompt.reshape(bp * N_MEM, D_MODEL)

    bias_p = _bias_table(rel_bias, CHUNK, SWA_BAND - CHUNK, SWA_BAND - CHUNK)
    bias_s = _bias_table(rel_bias, ss, n_cache + ss, n_cache)
    expand = _head_expansion()
    hist_p = jnp.zeros((bp, POOL_PAD, POOL_WIDTH), F32)
    hist_s = jnp.pad(state_pool, ((0, 0), (0, 0), (POOL_PAD - POOL_HIST, 0), (0, 0)))
    sinks = swa_sinks.astype(F32).reshape(depth, SWA_HEADS, 1, 1)
    sink_p = jnp.broadcast_to(sinks, (depth, SWA_HEADS, CHUNK, 128)).reshape(depth, SWA_HEADS * CHUNK, 128)
    sink_s = jnp.broadcast_to(sinks, (depth, SWA_HEADS, ss, 128)).reshape(depth, SWA_HEADS * ss, 128)
    norm3 = lambda g: g.reshape(depth, 1, D_MODEL)
    ffn_a_g, mix_g, mem_g, ffn_b_g = norm3(ffn_a_norm), norm3(mix_norm), norm3(mem_norm), norm3(ffn_b_norm)
    pool_s3 = pool_scale.reshape(depth, 1, POOL_WIDTH)
    cache_k4 = cache_swa_k.reshape(depth, bs, n_cache, SWA_KV_WIDTH)
    cache_v4 = cache_swa_v.reshape(depth, bs, n_cache, SWA_KV_WIDTH)
    cache_mk3 = cache_mem_k.reshape(depth, bs * N_MEM, MEM_WIDTH)
    cache_mv3 = cache_mem_v.reshape(depth, bs * N_MEM, MEM_WIDTH)

    y_pool = jnp.zeros((t_all, POOL_WIDTH), BF16)
    y_swa = jnp.zeros((t_all, SWA_WIDTH), BF16)
    y_mem = jnp.zeros((t_all, MEM_WIDTH), BF16)

    p_k, p_v, p_pool, p_mk, p_mv, s_k, s_v, s_pool = ([] for _ in range(8))
    fin3 = final_norm.reshape(1, 1, D_MODEL)
    hb, x, sumsq = _join(x_prompt.reshape(tp, D_MODEL), x_sample.reshape(ts, D_MODEL), ffn_a_g, 0)
    stream = (x, hb, sumsq)
    for l in range(depth):
        x, h, sumsq = _ffn_half(stream, ffn_a_w_in, ffn_a_w_out, l, mix_g, l)

        rs = _row_scale(sumsq)
        u = _proj(h, w_in, l, OFF_U, POOL_WIDTH, F32, rs)
        q = _proj(h, w_in, l, OFF_Q, SWA_WIDTH, BF16, rs, scale=SWA_HEAD_DIM ** -0.5)
        kv, kt, vt = _proj_kv(h, w_in, l, expand, rs)
        mq = _proj(h, w_in, l, OFF_MQ, MEM_WIDTH, BF16, rs, scale=MEM_HEAD_DIM ** -0.5)
        gates = _proj_staged(h, w_in, l, OFF_GATE, 3 * D_MODEL, F32, rs)

        y_pool = _pool(u, hist_p, pool_w, pool_s3, l, 0, bp, sp, 1, 0, t_all, prev=y_pool)
        y_pool = _pool(u, hist_s, pool_w, pool_s3, l, POOL_HIST, bs, ss, bs, tp, t_all, prev=y_pool)
        u_s = u[tp:].reshape(bs, ss, POOL_WIDTH)
        p_pool.append(u[:tp].reshape(bp, sp, POOL_WIDTH)[:, sp - POOL_HIST:])
        s_pool.append(jnp.concatenate([state_pool[l], u_s], axis=1)[:, ss:])

        y_swa = _swa_prompt(q, kt, vt, bias_p, sink_p, l, bp, n_chunk, y_swa)
        y_swa = _swa_sample(q, cache_k4, cache_v4, kv, expand, bias_s, sink_s, l, tp, bs, ss, y_swa)
        k_new, v_new = kv[:, :SWA_KV_WIDTH], kv[:, SWA_KV_WIDTH:]
        p_k.append(k_new[:tp].reshape(bp, sp, *kvh)[:, sp - SWA_WINDOW:])
        p_v.append(v_new[:tp].reshape(bp, sp, *kvh)[:, sp - SWA_WINDOW:])
        s_k.append(jnp.concatenate([cache_k4[l][:, ss:], k_new[tp:].reshape(bs, ss, SWA_KV_WIDTH)],
                                   axis=1).reshape(bs, n_cache, *kvh))
        s_v.append(jnp.concatenate([cache_v4[l][:, ss:], v_new[tp:].reshape(bs, ss, SWA_KV_WIDTH)],
                                   axis=1).reshape(bs, n_cache, *kvh))

        mkv = _proj(_rmsnorm(mem_x, (mem_g, l), BF16, bm=256), w_mem_kv, l, 0, 2 * MEM_WIDTH, F32, bm=256)
        y_mem = _mem_attend(mq, pl.BlockSpec((N_MEM, MEM_WIDTH), lambda b, s: (b, 0)),
                            pl.BlockSpec((N_MEM, MEM_WIDTH), lambda b, s: (b, 1)),
                            mkv, mkv, 0, bp, sp, 512, prev=y_mem)
        cache_spec = pl.BlockSpec((None, N_MEM, MEM_WIDTH), lambda b, s: (l, b, 0))
        y_mem = _mem_attend(mq, cache_spec, cache_spec, cache_mk3, cache_mv3, tp, bs, ss, ss,
                            prev=y_mem)
        p_mk.append(mkv[:, :MEM_WIDTH].reshape(bp, N_MEM, MEM_HEADS, MEM_HEAD_DIM))
        p_mv.append(mkv[:, MEM_WIDTH:].reshape(bp, N_MEM, MEM_HEADS, MEM_HEAD_DIM))

        m = _merge(gates, y_pool, y_swa, y_mem, w_branch_pool, w_branch_swa, w_branch_mem, l)
        stream = _proj_resid(m, w_out, l, x, ffn_b_g, l)

        g_next = (ffn_a_g, l + 1) if l + 1 < depth else (fin3, 0)
        stream = _ffn_half(stream, ffn_b_w_in, ffn_b_w_out, l, *g_next)

    x = stream[0]
    fin = (final_norm.reshape(1, D_MODEL), None)
    y_p = _rmsnorm(x, fin, F32, bm=256, row0=0, rows=tp)
    y_s = _rmsnorm(x, fin, F32, bm=256, row0=tp, rows=ts)
    return (y_p.reshape(bp, sp, D_MODEL), y_s.reshape(bs, ss, D_MODEL),
            jnp.stack(p_k), jnp.stack(p_v), jnp.stack(p_pool), jnp.stack(p_mk), jnp.stack(p_mv),
            jnp.stack(s_k), jnp.stack(s_v), jnp.stack(s_pool))
```

```python
import functools
import math

import jax
import jax.numpy as jnp
from jax import lax
from jax.experimental import pallas as pl
from jax.experimental.pallas import tpu as pltpu

F32 = jnp.float32
BF16 = jnp.bfloat16

D_MODEL = 4096
D_FF = 11008
CHUNK = 64
EPS = 1e-6
POOL_WINDOWS = (2, 4, 8, 16)
POOL_GROUP = 256
POOL_WIDTH = 1024
POOL_HIST = 15
POOL_PAD = 16
SWA_WINDOW = 128
SWA_HEAD_DIM = 64
SWA_HEADS = 32
SWA_KV_HEADS = 4
SWA_GROUP = 8
SWA_WIDTH = 2048
SWA_KV_WIDTH = 256
SWA_SPAN = SWA_GROUP * SWA_HEAD_DIM
SWA_BAND = 256
N_MEM = 256
MEM_HEADS = 4
MEM_WIDTH = 1024
MEM_HEAD_DIM = 256
REL_BUCKETS = 32
REL_MAX_DIST = 128
NEG = -1e30

OFF_U = 0
OFF_Q = OFF_U + POOL_WIDTH
OFF_KV = OFF_Q + SWA_WIDTH
OFF_MQ = OFF_KV + 2 * SWA_KV_WIDTH
OFF_GATE = OFF_MQ + MEM_WIDTH

VMEM_LIMIT_BYTES = 58 * 1024 * 1024
VMEM_LIMIT_STAGED_BYTES = 62 * 1024 * 1024
BM = 1088
WB = 512
NW = 1
BM_MERGE = 1088
FB = 256
BM_DOWN = 544
BN_DOWN = 512
LANES = 128
ROW_CHUNK = 272
CAST_ROWS = 512


def _cparams(n_axes, vmem_limit_bytes=VMEM_LIMIT_BYTES, flags=None):
    return pltpu.CompilerParams(
        dimension_semantics=("arbitrary",) * n_axes,
        vmem_limit_bytes=vmem_limit_bytes,
        flags=flags,
    )


def _any_spec():
    return pl.BlockSpec(memory_space=pl.ANY)


def _rms_kernel(x_ref, g_ref, o_ref):
    x = x_ref[...]
    ms = jnp.mean(x * x, axis=-1, keepdims=True)
    o_ref[...] = (x * lax.rsqrt(ms + EPS) * g_ref[...]).astype(o_ref.dtype)


def _rmsnorm(x, g, out_dtype, bm=544, row0=0, rows=None):
    d = x.shape[1]
    rows = x.shape[0] if rows is None else rows
    blk0 = row0 // bm
    g_arr, l = g
    if l is None:
        gspec = pl.BlockSpec((1, d), lambda i: (0, 0))
    else:
        gspec = pl.BlockSpec((None, 1, d), lambda i: (l, 0, 0))
    return pl.pallas_call(
        _rms_kernel,
        out_shape=jax.ShapeDtypeStruct((rows, d), out_dtype),
        grid=(rows // bm,),
        in_specs=[pl.BlockSpec((bm, d), lambda i: (blk0 + i, 0)), gspec],
        out_specs=pl.BlockSpec((bm, d), lambda i: (i, 0)),
        compiler_params=_cparams(1),
        name="rmsnorm",
    )(x, g_arr)


def _lane_partial_sumsq(x):
    sq = x * x
    part = sq[:, 0:LANES]
    for c0 in range(LANES, x.shape[1], LANES):
        part = part + sq[:, c0:c0 + LANES]
    return part


def _join_kernel(xp_ref, xs_ref, g_ref, hb_ref, x_ref, ss_ref, *, n_first):
    def emit(src_ref):
        x = src_ref[...]
        hb_ref[...] = (x * g_ref[...]).astype(hb_ref.dtype)
        x_ref[...] = x
        ss_ref[...] = _lane_partial_sumsq(x)

    i = pl.program_id(0)

    @pl.when(i < n_first)
    def _():
        emit(xp_ref)

    @pl.when(i >= n_first)
    def _():
        emit(xs_ref)


def _join(xp, xs, g, l, bm=256):
    d = xp.shape[1]
    n_first = xp.shape[0] // bm
    n_second = xs.shape[0] // bm
    rows = xp.shape[0] + xs.shape[0]
    out = pl.BlockSpec((bm, d), lambda i: (i, 0))
    return pl.pallas_call(
        functools.partial(_join_kernel, n_first=n_first),
        out_shape=(jax.ShapeDtypeStruct((rows, d), BF16), jax.ShapeDtypeStruct((rows, d), F32),
                   jax.ShapeDtypeStruct((1, rows, LANES), F32)),
        grid=(n_first + n_second,),
        in_specs=[pl.BlockSpec((bm, d), lambda i: (jnp.minimum(i, n_first - 1), 0)),
                  pl.BlockSpec((bm, d), lambda i: (jnp.maximum(i - n_first, 0), 0)),
                  pl.BlockSpec((None, 1, d), lambda i: (l, 0, 0))],
        out_specs=(out, out, pl.BlockSpec((None, bm, LANES), lambda i: (0, i, 0))),
        compiler_params=_cparams(1),
        name="join",
    )(xp, xs, g)


def _row_scale_kernel(ss_ref, r_ref):
    tot = jnp.sum(ss_ref[...], axis=0)
    ms = jnp.sum(tot, axis=1, keepdims=True) * (1.0 / D_MODEL)
    r_ref[...] = jnp.broadcast_to(lax.rsqrt(ms + EPS), r_ref.shape)


def _row_scale(ss, bm=BM):
    p, rows, _ = ss.shape
    return pl.pallas_call(
        _row_scale_kernel,
        out_shape=jax.ShapeDtypeStruct((rows, LANES), F32),
        grid=(rows // bm,),
        in_specs=[pl.BlockSpec((p, bm, LANES), lambda i: (0, i, 0))],
        out_specs=pl.BlockSpec((bm, LANES), lambda i: (i, 0)),
        compiler_params=_cparams(1),
        name="row_scale",
    )(ss)


def _stage_weights(w_refs, wb_ref):
    @pl.when(pl.program_id(1) == 0)
    def _():
        wd = w_refs[0].shape[1]
        for t, w in enumerate(w_refs):
            wb_ref[:, t * wd:(t + 1) * wd] = w[...].astype(BF16)


def _wspecs(k, wb, l, first, nw):
    return [pl.BlockSpec((None, k, wb), functools.partial(lambda j, i, t: (l, 0, first + nw * j + t), t=t))
            for t in range(nw)]


def _rspec(bm):
    return pl.BlockSpec((bm, LANES), lambda j, i: (i, 0))


def _row_chunks(bm):
    rc = ROW_CHUNK if bm % ROW_CHUNK == 0 else bm
    return [slice(r0, r0 + rc) for r0 in range(0, bm, rc)]


def _proj_kernel(*refs, nw, scale, scaled_rows):
    a_ref, w_refs = refs[0], refs[1:1 + nw]
    o_ref, wb_ref = refs[-2:]
    _stage_weights(w_refs, wb_ref)
    for rows in _row_chunks(a_ref.shape[0]):
        acc = jnp.dot(a_ref[rows, :], wb_ref[...], preferred_element_type=F32)
        if scaled_rows:
            acc = acc * refs[1 + nw][rows, 0:1]
        if scale != 1.0:
            acc = acc * scale
        o_ref[rows, :] = acc.astype(o_ref.dtype)


def _proj(a, w, l, col0, ncols, out_dtype, rscale=None, scale=1.0, bm=BM, wb=WB, nw=NW):
    m, k = a.shape
    bn = wb * nw
    extra_specs, extra = ([], []) if rscale is None else ([_rspec(bm)], [rscale])
    return pl.pallas_call(
        functools.partial(_proj_kernel, nw=nw, scale=scale, scaled_rows=rscale is not None),
        out_shape=jax.ShapeDtypeStruct((m, ncols), out_dtype),
        grid=(ncols // bn, m // bm),
        in_specs=[pl.BlockSpec((bm, k), lambda j, i: (i, 0))] + _wspecs(k, wb, l, col0 // wb, nw)
        + extra_specs,
        out_specs=pl.BlockSpec((bm, bn), lambda j, i: (i, j)),
        scratch_shapes=[pltpu.VMEM((k, bn), BF16)],
        compiler_params=_cparams(2),
        name="proj",
    )(a, *([w] * nw), *extra)


def _proj_kv_kernel(a_ref, w_ref, e_ref, r_ref, kv_ref, kt_ref, vt_ref, wb_ref):
    _stage_weights((w_ref,), wb_ref)
    acc = jnp.dot(a_ref[...], wb_ref[...], preferred_element_type=F32) * r_ref[:, 0:1]
    kv_ref[...] = acc
    kv = acc.astype(BF16)
    kt_ref[...] = jnp.dot(kv[:, :SWA_KV_WIDTH], e_ref[...], preferred_element_type=F32).astype(BF16)
    vt_ref[...] = jnp.dot(kv[:, SWA_KV_WIDTH:], e_ref[...], preferred_element_type=F32).astype(BF16)


def _proj_kv(a, w, l, expand, rscale, bm=BM // 2):
    m, k = a.shape
    n = 2 * SWA_KV_WIDTH
    row = lambda width: pl.BlockSpec((bm, width), lambda j, i: (i, 0))
    return pl.pallas_call(
        _proj_kv_kernel,
        out_shape=(jax.ShapeDtypeStruct((m, n), F32),
                   jax.ShapeDtypeStruct((m, SWA_WIDTH), BF16),
                   jax.ShapeDtypeStruct((m, SWA_WIDTH), BF16)),
        grid=(1, m // bm),
        in_specs=[row(k)] + _wspecs(k, n, l, OFF_KV // n, 1)
        + [pl.BlockSpec((SWA_KV_WIDTH, SWA_WIDTH), lambda j, i: (0, 0)), _rspec(bm)],
        out_specs=(row(n), row(SWA_WIDTH), row(SWA_WIDTH)),
        scratch_shapes=[pltpu.VMEM((k, n), BF16)],
        compiler_params=_cparams(2),
        name="proj_kv",
    )(a, w, expand, rscale)


def _emit_stream(rows, x_new, g_ref, o_ref, hb_ref):
    o_ref[rows, :] = x_new
    hb_ref[rows, :] = (x_new * g_ref[...]).astype(hb_ref.dtype)


def _proj_resid_kernel(*refs, nw):
    a_ref, w_refs = refs[0], refs[1:1 + nw]
    r_ref, g_ref, o_ref, hb_ref, ss_ref, wb_ref = refs[1 + nw:]
    _stage_weights(w_refs, wb_ref)
    for rows in _row_chunks(a_ref.shape[0]):
        x_new = r_ref[rows, :] + jnp.dot(a_ref[rows, :], wb_ref[...], preferred_element_type=F32)
        _emit_stream(rows, x_new, g_ref, o_ref, hb_ref)
        ss_ref[rows, :] = _lane_partial_sumsq(x_new)


def _proj_resid(a, w, l, resid, g, gl, bm=BM, wb=WB, nw=NW):
    m, k = a.shape
    n = w.shape[2]
    bn = wb * nw
    nj = n // bn
    tile = pl.BlockSpec((bm, bn), lambda j, i: (i, j))
    return pl.pallas_call(
        functools.partial(_proj_resid_kernel, nw=nw),
        out_shape=(jax.ShapeDtypeStruct((m, n), F32), jax.ShapeDtypeStruct((m, n), BF16),
                   jax.ShapeDtypeStruct((nj, m, LANES), F32)),
        grid=(nj, m // bm),
        in_specs=[pl.BlockSpec((bm, k), lambda j, i: (i, 0))] + _wspecs(k, wb, l, 0, nw)
        + [tile, pl.BlockSpec((None, 1, bn), lambda j, i: (gl, 0, j))],
        out_specs=(tile, tile, pl.BlockSpec((None, bm, LANES), lambda j, i: (j, i, 0))),
        scratch_shapes=[pltpu.VMEM((k, bn), BF16)],
        compiler_params=_cparams(2),
        name="proj_resid",
    )(a, *([w] * nw), resid, g)


def _staged_weights(i, j, nj, start_copies, wait_copies, stage_ref, wb_ref):
    @pl.when(i == 0)
    def _():
        @pl.when(j == 0)
        def _():
            start_copies(0)

        wait_copies(j)
        for r0 in range(0, wb_ref.shape[0], CAST_ROWS):
            wb_ref[r0:r0 + CAST_ROWS, :] = stage_ref[r0:r0 + CAST_ROWS, :].astype(BF16)

        @pl.when(j + 1 < nj)
        def _():
            start_copies(j + 1)


def _proj_staged_kernel(a_ref, w_hbm, r_ref, o_ref, stage_ref, wb_ref, sem, *, l, nj, col0, scale):
    j = pl.program_id(0)
    i = pl.program_id(1)
    bn = o_ref.shape[1]

    def copy(jj):
        col = pl.multiple_of(col0 + jj * bn, WB)
        return pltpu.make_async_copy(w_hbm.at[l, :, pl.ds(col, bn)], stage_ref, sem.at[0])

    _staged_weights(i, j, nj, lambda jj: copy(jj).start(), lambda jj: copy(jj).wait(),
                    stage_ref, wb_ref)
    rc = a_ref.shape[0] // 4
    for r0 in range(0, a_ref.shape[0], rc):
        acc = jnp.dot(a_ref[r0:r0 + rc, :], wb_ref[...], preferred_element_type=F32)
        acc = acc * r_ref[r0:r0 + rc, 0:1]
        if scale != 1.0:
            acc = acc * scale
        o_ref[r0:r0 + rc, :] = acc.astype(o_ref.dtype)


def _proj_staged(a, w, l, col0, ncols, out_dtype, rscale, scale=1.0, bm=BM, bn=2 * WB):
    m, k = a.shape
    nj = ncols // bn
    assert col0 % WB == 0 and ncols % bn == 0
    return pl.pallas_call(
        functools.partial(_proj_staged_kernel, l=l, nj=nj, col0=col0, scale=scale),
        out_shape=jax.ShapeDtypeStruct((m, ncols), out_dtype),
        grid=(nj, m // bm),
        in_specs=[pl.BlockSpec((bm, k), lambda j, i: (i, 0)), _any_spec(), _rspec(bm)],
        out_specs=pl.BlockSpec((bm, bn), lambda j, i: (i, j)),
        scratch_shapes=[pltpu.VMEM((k, bn), F32), pltpu.VMEM((k, bn), BF16),
                        pltpu.SemaphoreType.DMA((1,))],
        compiler_params=_cparams(2, VMEM_LIMIT_STAGED_BYTES),
        name="proj_staged",
    )(a, w, rscale)


def _swiglu_kernel(a_ref, w_hbm, wo_ref, r_ref, o_ref, wob_ref, stage_ref, wb_ref, sem,
                   *, l, nj, fn, tail):
    j = pl.program_id(0)
    i = pl.program_id(1)

    def copies(jj, width):
        col = pl.multiple_of(jj * fn, fn)
        return (pltpu.make_async_copy(w_hbm.at[l, :, pl.ds(col, width)],
                                      stage_ref.at[:, :width], sem.at[0]),
                pltpu.make_async_copy(w_hbm.at[l, :, pl.ds(D_FF + col, width)],
                                      stage_ref.at[:, fn:fn + width], sem.at[1]))

    def each(jj, method):
        @pl.when(jj < nj - 1)
        def _():
            for c in copies(jj, fn):
                getattr(c, method)()

        @pl.when(jj == nj - 1)
        def _():
            for c in copies(jj, tail):
                getattr(c, method)()

    _staged_weights(i, j, nj, lambda jj: each(jj, "start"), lambda jj: each(jj, "wait"),
                    stage_ref, wb_ref)
    wob_ref[...] = wo_ref[...].astype(BF16)

    def swiglu(width):
        rc = a_ref.shape[0] // 4
        for r0 in range(0, a_ref.shape[0], rc):
            a = a_ref[r0:r0 + rc, :]
            rs = r_ref[r0:r0 + rc, 0:1]
            if width == fn:
                acc = jnp.dot(a, wb_ref[...], preferred_element_type=F32) * rs
                gate, up = acc[:, :fn], acc[:, fn:]
            else:
                gate = jnp.dot(a, wb_ref[:, :width], preferred_element_type=F32) * rs
                up = jnp.dot(a, wb_ref[:, fn:fn + width], preferred_element_type=F32) * rs
                o_ref[r0:r0 + rc, width:] = jnp.zeros((rc, fn - width), o_ref.dtype)
            o_ref[r0:r0 + rc, :width] = (gate * jax.nn.sigmoid(gate) * up).astype(o_ref.dtype)

    @pl.when(j < nj - 1)
    def _():
        swiglu(fn)

    @pl.when(j == nj - 1)
    def _():
        swiglu(tail)


def _ffn_up(a, w_in, w_out, l, rscale, bm=BM, fn=2 * FB):
    m, k = a.shape
    nj = pl.cdiv(D_FF, fn)
    tail = D_FF - (nj - 1) * fn
    ni = m // bm
    n_out = w_out.shape[2]
    oc = n_out // ni
    return pl.pallas_call(
        functools.partial(_swiglu_kernel, l=l, nj=nj, fn=fn, tail=tail),
        out_shape=(jax.ShapeDtypeStruct((m, nj * fn), BF16),
                   jax.ShapeDtypeStruct((D_FF, n_out), BF16)),
        grid=(nj, ni),
        in_specs=[pl.BlockSpec((bm, k), lambda j, i: (i, 0)),
                  _any_spec(),
                  pl.BlockSpec((None, fn, oc), lambda j, i: (l, j, i)),
                  _rspec(bm)],
        out_specs=(pl.BlockSpec((bm, fn), lambda j, i: (i, j)),
                   pl.BlockSpec((fn, oc), lambda j, i: (j, i))),
        scratch_shapes=[pltpu.VMEM((k, 2 * fn), F32), pltpu.VMEM((k, 2 * fn), BF16),
                        pltpu.SemaphoreType.DMA((2,))],
        compiler_params=_cparams(2, VMEM_LIMIT_STAGED_BYTES),
        name="ffn_up",
    )(a, w_in, w_out, rscale)


def _ffn_down_kernel(a_ref, w_ref, r_ref, g_ref, o_ref, hb_ref, ss_ref):
    @pl.when(pl.program_id(1) == 0)
    def _():
        ss_ref[...] = jnp.zeros_like(ss_ref)

    for rows in _row_chunks(a_ref.shape[0]):
        x_new = r_ref[rows, :] + 0.5 * jnp.dot(a_ref[rows, :], w_ref[...], preferred_element_type=F32)
        _emit_stream(rows, x_new, g_ref, o_ref, hb_ref)
        ss_ref[rows, :] += _lane_partial_sumsq(x_new)


def _ffn_down(a, w_bf16, resid, g, gl, bm=BM_DOWN, bn=BN_DOWN):
    m = a.shape[0]
    k, n = w_bf16.shape
    tile = pl.BlockSpec((bm, bn), lambda i, j: (i, j))
    return pl.pallas_call(
        _ffn_down_kernel,
        out_shape=(jax.ShapeDtypeStruct((m, n), F32), jax.ShapeDtypeStruct((m, n), BF16),
                   jax.ShapeDtypeStruct((1, m, LANES), F32)),
        grid=(m // bm, n // bn),
        in_specs=[pl.BlockSpec((bm, k), lambda i, j: (i, 0)),
                  pl.BlockSpec((k, bn), lambda i, j: (0, j)), tile,
                  pl.BlockSpec((None, 1, bn), lambda i, j: (gl, 0, j))],
        out_specs=(tile, tile, pl.BlockSpec((None, bm, LANES), lambda i, j: (0, i, 0))),
        compiler_params=_cparams(2),
        name="ffn_down",
    )(a, w_bf16, resid, g)


def _merge_kernel(yp_ref, ya_ref, ym_ref, wp_ref, wa_ref, wm_ref, gp_ref, ga_ref, gm_ref,
                  o_ref, wpb_ref, wab_ref, wmb_ref):
    @pl.when(pl.program_id(1) == 0)
    def _():
        wpb_ref[...] = wp_ref[...].astype(BF16)
        wab_ref[...] = wa_ref[...].astype(BF16)
        wmb_ref[...] = wm_ref[...].astype(BF16)

    for rows in _row_chunks(yp_ref.shape[0]):
        m = jax.nn.sigmoid(gp_ref[rows, :]) * jnp.dot(yp_ref[rows, :], wpb_ref[...],
                                                      preferred_element_type=F32)
        m = m + jax.nn.sigmoid(ga_ref[rows, :]) * jnp.dot(ya_ref[rows, :], wab_ref[...],
                                                          preferred_element_type=F32)
        m = m + jax.nn.sigmoid(gm_ref[rows, :]) * jnp.dot(ym_ref[rows, :], wmb_ref[...],
                                                          preferred_element_type=F32)
        o_ref[rows, :] = m.astype(o_ref.dtype)


def _merge(gates, y_pool, y_swa, y_mem, w_bp, w_bs, w_bm, l, bm=BM_MERGE, bn=WB):
    m = gates.shape[0]
    nb = D_MODEL // bn
    act = lambda k: pl.BlockSpec((bm, k), lambda j, i: (i, 0))
    wgt = lambda k: pl.BlockSpec((None, k, bn), lambda j, i: (l, 0, j))
    gate = lambda t: pl.BlockSpec((bm, bn), lambda j, i: (i, j + t * nb))
    return pl.pallas_call(
        _merge_kernel,
        out_shape=jax.ShapeDtypeStruct((m, D_MODEL), BF16),
        grid=(nb, m // bm),
        in_specs=[act(POOL_WIDTH), act(SWA_WIDTH), act(MEM_WIDTH),
                  wgt(POOL_WIDTH), wgt(SWA_WIDTH), wgt(MEM_WIDTH),
                  gate(0), gate(1), gate(2)],
        out_specs=pl.BlockSpec((bm, bn), lambda j, i: (i, j)),
        scratch_shapes=[pltpu.VMEM((POOL_WIDTH, bn), BF16), pltpu.VMEM((SWA_WIDTH, bn), BF16),
                        pltpu.VMEM((MEM_WIDTH, bn), BF16)],
        compiler_params=_cparams(2),
        name="merge",
    )(y_pool, y_swa, y_mem, w_bp, w_bs, w_bm, gates, gates, gates)


def _pool_kernel(*refs, bb, s_len, n_hist, rc):
    u_ref, h_ref, w_ref, s_ref = refs[:4]
    o_ref, full_ref = refs[-2:]
    full_ref[:, 0:POOL_PAD, :] = h_ref[...]
    full_ref[:, POOL_PAD:, :] = u_ref[...].reshape(bb, s_len, POOL_WIDTH)
    for g, win in enumerate(POOL_WINDOWS):
        lanes = slice(g * POOL_GROUP, (g + 1) * POOL_GROUP)
        wg = w_ref[g].astype(BF16)
        sc = s_ref[:, lanes]
        for r0 in range(0, s_len, rc):
            base = POOL_PAD + r0
            cur = full_ref[:, base:base + rc, lanes]
            acc = cur
            for d in range(1, win):
                acc = acc + full_ref[:, base - d:base - d + rc, lanes]
            t = lax.broadcasted_iota(jnp.int32, (bb, rc, POOL_GROUP), 1) + (r0 + n_hist + 1)
            cnt = jnp.minimum(t, win).astype(F32)
            mixed = (acc / cnt - cur).reshape(bb * rc, POOL_GROUP).astype(BF16)
            y = jnp.dot(mixed, wg, preferred_element_type=F32) * sc
            o_ref[r0:r0 + bb * rc, lanes] = y.astype(o_ref.dtype)


def _pool(u, hist, pool_w, pool_scale, l, n_hist, n_batch, s_len, bb, row0, out_rows, prev=None):
    rows = bb * s_len
    blk0 = row0 // rows
    rc = min(s_len, 256)
    assert bb == 1 or rc == s_len
    if hist.ndim == 4:
        hspec = pl.BlockSpec((None, bb, POOL_PAD, POOL_WIDTH), lambda b: (l, b, 0, 0))
    else:
        hspec = pl.BlockSpec((bb, POOL_PAD, POOL_WIDTH), lambda b: (b, 0, 0))
    in_specs = [pl.BlockSpec((rows, POOL_WIDTH), lambda b: (blk0 + b, 0)), hspec,
                pl.BlockSpec((None, len(POOL_WINDOWS), POOL_GROUP, POOL_GROUP), lambda b: (l, 0, 0, 0)),
                pl.BlockSpec((None, 1, POOL_WIDTH), lambda b: (l, 0, 0))]
    args = [u, hist, pool_w, pool_scale]
    aliases = {}
    if prev is not None:
        in_specs.append(_any_spec())
        args.append(prev)
        aliases = {4: 0}
    return pl.pallas_call(
        functools.partial(_pool_kernel, bb=bb, s_len=s_len, n_hist=n_hist, rc=rc),
        out_shape=jax.ShapeDtypeStruct((out_rows, POOL_WIDTH), BF16),
        grid=(n_batch // bb,),
        in_specs=in_specs,
        out_specs=pl.BlockSpec((rows, POOL_WIDTH), lambda b: (blk0 + b, 0)),
        scratch_shapes=[pltpu.VMEM((bb, POOL_PAD + s_len, POOL_WIDTH), F32)],
        input_output_aliases=aliases,
        compiler_params=_cparams(1),
        name="pool",
    )(*args)


def _rel_bucket(rel):
    half = REL_BUCKETS // 2
    max_exact = half // 2
    n = jnp.abs(rel)
    nf = jnp.maximum(n, 1).astype(jnp.float32)
    large = max_exact + (jnp.log(nf / max_exact) / math.log(REL_MAX_DIST / max_exact)
                         * (half - max_exact)).astype(jnp.int32)
    large = jnp.minimum(large, half - 1)
    return jnp.where(rel > 0, half, 0) + jnp.where(n < max_exact, n, large)


def _bias_kernel(tab_ref, idx_ref, o_ref):
    h = pl.program_id(0)
    idx = idx_ref[...]
    acc = jnp.where(idx < 0, NEG, 0.0).astype(F32)
    for b in range(REL_BUCKETS):
        acc = acc + jnp.where(idx == b, tab_ref[b, h], 0.0)
    o_ref[0] = acc


def _bias_table(table, sq, n_keys, key_zero):
    band_lo = key_zero - SWA_WINDOW
    j = jnp.arange(SWA_BAND)[None, :]
    qi = jnp.arange(sq)[:, None]
    idx = _rel_bucket(j - key_zero - qi)
    idx = jnp.where((j >= band_lo) & (j < band_lo + n_keys), idx, -1).astype(jnp.int32)
    return pl.pallas_call(
        _bias_kernel,
        out_shape=jax.ShapeDtypeStruct((SWA_HEADS, sq, SWA_BAND), F32),
        grid=(SWA_HEADS,),
        in_specs=[pl.BlockSpec(memory_space=pltpu.SMEM),
                  pl.BlockSpec((sq, SWA_BAND), lambda h: (0, 0))],
        out_specs=pl.BlockSpec((1, sq, SWA_BAND), lambda h: (h, 0, 0)),
        compiler_params=_cparams(1),
        name="rel_bias",
    )(table, idx)


def _swa_core(q, kband, vband, b_ref, s_ref, lo, sq):
    rows = SWA_GROUP * sq
    lane_grp = lax.shift_right_logical(lax.broadcasted_iota(jnp.int32, (sq, SWA_SPAN), 1), 6)
    col_ok = None
    if lo is not None:
        col_ok = lax.broadcasted_iota(jnp.int32, (SWA_HEADS * sq, SWA_BAND), 1) >= lo
    parts = []
    for h in range(SWA_KV_HEADS):
        lanes = slice(h * SWA_SPAN, (h + 1) * SWA_SPAN)
        qs = q[:, lanes]
        qstack = jnp.concatenate(
            [jnp.where(lane_grp == g, qs, jnp.zeros_like(qs)) for g in range(SWA_GROUP)], axis=0)
        parts.append(lax.dot_general(qstack, kband[:, lanes], (((1,), (1,)), ((), ())),
                                     preferred_element_type=F32))
    s = jnp.concatenate(parts, axis=0) + b_ref[...].reshape(SWA_HEADS * sq, SWA_BAND)
    if col_ok is not None:
        s = jnp.where(col_ok, s, NEG)
    sk = s_ref[:, 0:1]
    m = jnp.maximum(jnp.max(s, axis=1, keepdims=True), sk)
    e = jnp.exp(s - m)
    z = jnp.sum(e, axis=1, keepdims=True) + jnp.exp(sk - m)
    p = (e / z).astype(BF16)
    outs = []
    for h in range(SWA_KV_HEADS):
        lanes = slice(h * SWA_SPAN, (h + 1) * SWA_SPAN)
        r = jnp.dot(p[h * rows:(h + 1) * rows], vband[:, lanes], preferred_element_type=F32)
        o = jnp.where(lane_grp == 0, r[0:sq], 0.0)
        for g in range(1, SWA_GROUP):
            o = o + jnp.where(lane_grp == g, r[g * sq:(g + 1) * sq], 0.0)
        outs.append(o)
    return jnp.concatenate(outs, axis=1)


def _swa_prompt_kernel(*refs, n_kb):
    q_ref = refs[0]
    k_refs = refs[1:1 + n_kb]
    v_refs = refs[1 + n_kb:1 + 2 * n_kb]
    b_ref, s_ref = refs[1 + 2 * n_kb:3 + 2 * n_kb]
    o_ref = refs[-1]
    sq = q_ref.shape[0]
    kband = jnp.concatenate([r[...] for r in k_refs], axis=0)
    vband = jnp.concatenate([r[...] for r in v_refs], axis=0)
    lo = jnp.maximum(0, (1 - pl.program_id(1)) * 2 * CHUNK)
    o = _swa_core(q_ref[...], kband, vband, b_ref, s_ref, lo, sq)
    o_ref[...] = o.astype(o_ref.dtype)


def _swa_prompt(q, kt, vt, bias, sink_cols, l, n_batch, n_chunk, prev):
    sq = 2 * CHUNK
    n_kb = SWA_BAND // CHUNK
    n_pair = n_chunk // 2
    qspec = pl.BlockSpec((sq, SWA_WIDTH), lambda b, c: (b * n_pair + c, 0))

    def band(d):
        return pl.BlockSpec((CHUNK, SWA_WIDTH),
                            lambda b, c: (b * n_chunk + jnp.maximum(2 * c - 2 + d, 0), 0))

    return pl.pallas_call(
        functools.partial(_swa_prompt_kernel, n_kb=n_kb),
        out_shape=jax.ShapeDtypeStruct(prev.shape, BF16),
        grid=(n_batch, n_pair),
        in_specs=[qspec] + [band(d) for d in range(n_kb)] * 2
        + [pl.BlockSpec((SWA_HEADS, sq, SWA_BAND), lambda b, c: (0, 0, 0)),
           pl.BlockSpec((None, SWA_HEADS * sq, 128), lambda b, c: (l, 0, 0)),
           _any_spec()],
        out_specs=qspec,
        input_output_aliases={3 + 2 * n_kb: 0},
        compiler_params=_cparams(2),
        name="swa_prompt",
    )(q, *([kt] * n_kb), *([vt] * n_kb), bias, sink_cols, prev)


def _swa_sample_kernel(q_ref, ck_ref, cv_ref, kvn_ref, e_ref, b_ref, s_ref, prev_ref, o_ref):
    del prev_ref
    sq = q_ref.shape[0]
    n_cache = ck_ref.shape[0]
    kvn = kvn_ref[...]
    zpad = jnp.zeros((SWA_BAND - n_cache - sq, SWA_KV_WIDTH), F32)
    kk = jnp.concatenate([ck_ref[...], kvn[:, :SWA_KV_WIDTH], zpad], axis=0).astype(BF16)
    vv = jnp.concatenate([cv_ref[...], kvn[:, SWA_KV_WIDTH:], zpad], axis=0).astype(BF16)
    kband = jnp.dot(kk, e_ref[...], preferred_element_type=F32).astype(BF16)
    vband = jnp.dot(vv, e_ref[...], preferred_element_type=F32).astype(BF16)
    o = _swa_core(q_ref[...], kband, vband, b_ref, s_ref, None, sq)
    o_ref[...] = o.astype(o_ref.dtype)


def _swa_sample(q, cache_k, cache_v, kv, expand, bias, sink_cols, l, row0, n_batch, sq, prev):
    blk0 = row0 // sq
    n_cache = cache_k.shape[2]
    cspec = pl.BlockSpec((None, None, n_cache, SWA_KV_WIDTH), lambda b: (l, b, 0, 0))
    return pl.pallas_call(
        _swa_sample_kernel,
        out_shape=jax.ShapeDtypeStruct(prev.shape, BF16),
        grid=(n_batch,),
        in_specs=[pl.BlockSpec((sq, SWA_WIDTH), lambda b: (blk0 + b, 0)), cspec, cspec,
                  pl.BlockSpec((sq, 2 * SWA_KV_WIDTH), lambda b: (blk0 + b, 0)),
                  pl.BlockSpec((SWA_KV_WIDTH, SWA_WIDTH), lambda b: (0, 0)),
                  pl.BlockSpec((SWA_HEADS, sq, SWA_BAND), lambda b: (0, 0, 0)),
                  pl.BlockSpec((None, SWA_HEADS * sq, 128), lambda b: (l, 0, 0)),
                  _any_spec()],
        out_specs=pl.BlockSpec((sq, SWA_WIDTH), lambda b: (blk0 + b, 0)),
        input_output_aliases={7: 0},
        compiler_params=_cparams(1),
        name="swa_sample",
    )(q, cache_k, cache_v, kv, expand, bias, sink_cols, prev)


def _mem_kernel(*refs):
    q_ref, k_ref, v_ref = refs[:3]
    o_ref = refs[-1]
    for h in range(MEM_HEADS):
        lanes = slice(h * MEM_HEAD_DIM, (h + 1) * MEM_HEAD_DIM)
        kh = k_ref[:, lanes].astype(BF16)
        vh = v_ref[:, lanes].astype(BF16)
        s = lax.dot_general(q_ref[:, lanes], kh, (((1,), (1,)), ((), ())),
                            preferred_element_type=F32)
        m = jnp.max(s, axis=1, keepdims=True)
        e = jnp.exp(s - m)
        p = (e / jnp.sum(e, axis=1, keepdims=True)).astype(BF16)
        o_ref[:, lanes] = jnp.dot(p, vh, preferred_element_type=F32).astype(o_ref.dtype)


def _mem_attend(mq, kspec, vspec, k, v, row0, n_batch, s_len, bq, prev=None):
    per = s_len // bq
    blk0 = row0 // bq
    qspec = pl.BlockSpec((bq, MEM_WIDTH), lambda b, s: (blk0 + b * per + s, 0))
    in_specs = [qspec, kspec, vspec]
    args = [mq, k, v]
    aliases = {}
    if prev is not None:
        in_specs.append(_any_spec())
        args.append(prev)
        aliases = {3: 0}
    return pl.pallas_call(
        _mem_kernel,
        out_shape=jax.ShapeDtypeStruct(mq.shape, BF16),
        grid=(n_batch, per),
        in_specs=in_specs,
        out_specs=qspec,
        input_output_aliases=aliases,
        compiler_params=_cparams(2),
        name="mem_attend",
    )(*args)


def _head_expansion():
    src = jnp.arange(SWA_KV_WIDTH)
    dst = jnp.arange(SWA_WIDTH)
    dst_src = (dst // SWA_SPAN) * SWA_HEAD_DIM + dst % SWA_HEAD_DIM
    return (src[:, None] == dst_src[None, :]).astype(BF16)


def _ffn_half(stream, w_in, w_out, l, g_next, gl_next):
    x, hb, ss = stream
    hh, w_out_bf16 = _ffn_up(hb, w_in, w_out, l, _row_scale(ss))
    return _ffn_down(hh, w_out_bf16, x, g_next, gl_next)


def kernel(x_prompt, x_sample, cache_swa_k, cache_swa_v, state_pool, cache_mem_k, cache_mem_v,
           mem_prompt, ffn_a_norm, ffn_a_w_in, ffn_a_w_out, mix_norm, mem_norm, w_in, pool_w,
           pool_scale, swa_sinks, rel_bias, w_mem_kv, w_branch_pool, w_branch_swa, w_branch_mem,
           w_out, ffn_b_norm, ffn_b_w_in, ffn_b_w_out, final_norm):
    bp, sp, _ = x_prompt.shape
    bs, ss, _ = x_sample.shape
    depth = w_in.shape[0]
    tp = bp * sp
    ts = bs * ss
    t_all = tp + ts
    n_chunk = sp // CHUNK
    n_cache = cache_swa_k.shape[2]
    kvh = (SWA_KV_HEADS, SWA_HEAD_DIM)

    mem_x = mem_prompt.reshape(bp * N_MEM, D_MODEL)

    bias_p = jnp.concatenate([_bias_table(rel_bias, CHUNK, SWA_BAND - CHUNK, SWA_WINDOW),
                              _bias_table(rel_bias, CHUNK, SWA_BAND - CHUNK, SWA_BAND - CHUNK)], axis=1)
    bias_s = _bias_table(rel_bias, ss, n_cache + ss, n_cache)
    expand = _head_expansion()
    hist_p = jnp.zeros((bp, POOL_PAD, POOL_WIDTH), F32)
    hist_s = jnp.pad(state_pool, ((0, 0), (0, 0), (POOL_PAD - POOL_HIST, 0), (0, 0)))
    sinks = swa_sinks.astype(F32).reshape(depth, SWA_HEADS, 1, 1)
    sink_p = jnp.broadcast_to(sinks, (depth, SWA_HEADS, 2 * CHUNK, 128)).reshape(
        depth, SWA_HEADS * 2 * CHUNK, 128)
    sink_s = jnp.broadcast_to(sinks, (depth, SWA_HEADS, ss, 128)).reshape(depth, SWA_HEADS * ss, 128)
    norm3 = lambda g: g.reshape(depth, 1, D_MODEL)
    ffn_a_g, mix_g, mem_g, ffn_b_g = norm3(ffn_a_norm), norm3(mix_norm), norm3(mem_norm), norm3(ffn_b_norm)
    pool_s3 = pool_scale.reshape(depth, 1, POOL_WIDTH)
    cache_k4 = cache_swa_k.reshape(depth, bs, n_cache, SWA_KV_WIDTH)
    cache_v4 = cache_swa_v.reshape(depth, bs, n_cache, SWA_KV_WIDTH)
    cache_mk3 = cache_mem_k.reshape(depth, bs * N_MEM, MEM_WIDTH)
    cache_mv3 = cache_mem_v.reshape(depth, bs * N_MEM, MEM_WIDTH)

    y_pool = jnp.zeros((t_all, POOL_WIDTH), BF16)
    y_swa = jnp.zeros((t_all, SWA_WIDTH), BF16)
    y_mem = jnp.zeros((t_all, MEM_WIDTH), BF16)

    p_k, p_v, p_pool, p_mk, p_mv, s_k, s_v, s_pool = ([] for _ in range(8))
    fin3 = final_norm.reshape(1, 1, D_MODEL)
    hb, x, sumsq = _join(x_prompt.reshape(tp, D_MODEL), x_sample.reshape(ts, D_MODEL), ffn_a_g, 0)
    stream = (x, hb, sumsq)
    for l in range(depth):
        x, h, sumsq = _ffn_half(stream, ffn_a_w_in, ffn_a_w_out, l, mix_g, l)

        rs = _row_scale(sumsq)
        u = _proj(h, w_in, l, OFF_U, POOL_WIDTH, F32, rs)
        q = _proj(h, w_in, l, OFF_Q, SWA_WIDTH, BF16, rs, scale=SWA_HEAD_DIM ** -0.5)
        kv, kt, vt = _proj_kv(h, w_in, l, expand, rs)
        mq = _proj(h, w_in, l, OFF_MQ, MEM_WIDTH, BF16, rs, scale=MEM_HEAD_DIM ** -0.5)
        gates = _proj_staged(h, w_in, l, OFF_GATE, 3 * D_MODEL, F32, rs)

        y_pool = _pool(u, hist_p, pool_w, pool_s3, l, 0, bp, sp, 1, 0, t_all, prev=y_pool)
        y_pool = _pool(u, hist_s, pool_w, pool_s3, l, POOL_HIST, bs, ss, bs, tp, t_all, prev=y_pool)
        u_s = u[tp:].reshape(bs, ss, POOL_WIDTH)
        p_pool.append(u[:tp].reshape(bp, sp, POOL_WIDTH)[:, sp - POOL_HIST:])
        s_pool.append(jnp.concatenate([state_pool[l], u_s], axis=1)[:, ss:])

        y_swa = _swa_prompt(q, kt, vt, bias_p, sink_p, l, bp, n_chunk, y_swa)
        y_swa = _swa_sample(q, cache_k4, cache_v4, kv, expand, bias_s, sink_s, l, tp, bs, ss, y_swa)
        k_new, v_new = kv[:, :SWA_KV_WIDTH], kv[:, SWA_KV_WIDTH:]
        p_k.append(k_new[:tp].reshape(bp, sp, *kvh)[:, sp - SWA_WINDOW:])
        p_v.append(v_new[:tp].reshape(bp, sp, *kvh)[:, sp - SWA_WINDOW:])
        s_k.append(jnp.concatenate([cache_k4[l][:, ss:], k_new[tp:].reshape(bs, ss, SWA_KV_WIDTH)],
                                   axis=1).reshape(bs, n_cache, *kvh))
        s_v.append(jnp.concatenate([cache_v4[l][:, ss:], v_new[tp:].reshape(bs, ss, SWA_KV_WIDTH)],
                                   axis=1).reshape(bs, n_cache, *kvh))

        mkv = _proj(_rmsnorm(mem_x, (mem_g, l), BF16, bm=256), w_mem_kv, l, 0, 2 * MEM_WIDTH, F32, bm=256)
        y_mem = _mem_attend(mq, pl.BlockSpec((N_MEM, MEM_WIDTH), lambda b, s: (b, 0)),
                            pl.BlockSpec((N_MEM, MEM_WIDTH), lambda b, s: (b, 1)),
                            mkv, mkv, 0, bp, sp, 512, prev=y_mem)
        cache_spec = pl.BlockSpec((None, N_MEM, MEM_WIDTH), lambda b, s: (l, b, 0))
        y_mem = _mem_attend(mq, cache_spec, cache_spec, cache_mk3, cache_mv3, tp, bs, ss, ss,
                            prev=y_mem)
        p_mk.append(mkv[:, :MEM_WIDTH].reshape(bp, N_MEM, MEM_HEADS, MEM_HEAD_DIM))
        p_mv.append(mkv[:, MEM_WIDTH:].reshape(bp, N_MEM, MEM_HEADS, MEM_HEAD_DIM))

        m = _merge(gates, y_pool, y_swa, y_mem, w_branch_pool, w_branch_swa, w_branch_mem, l)
        stream = _proj_resid(m, w_out, l, x, ffn_b_g, l)

        g_next = (ffn_a_g, l + 1) if l + 1 < depth else (fin3, 0)
        stream = _ffn_half(stream, ffn_b_w_in, ffn_b_w_out, l, *g_next)

    x = stream[0]
    fin = (final_norm.reshape(1, D_MODEL), None)
    y_p = _rmsnorm(x, fin, F32, bm=256, row0=0, rows=tp)
    y_s = _rmsnorm(x, fin, F32, bm=256, row0=tp, rows=ts)
    return (y_p.reshape(bp, sp, D_MODEL), y_s.reshape(bs, ss, D_MODEL),
            jnp.stack(p_k), jnp.stack(p_v), jnp.stack(p_pool), jnp.stack(p_mk), jnp.stack(p_mv),
            jnp.stack(s_k), jnp.stack(s_v), jnp.stack(s_pool))
```
